```python
import jax, jax.numpy as jnp
from jax import lax
import numpy as np

D_MODEL = 1024
BATCH = 2
SEQ = 16384
DEPTH = 2

HEAD_DIM = 64
FOX_HEADS = 6
DIL_KV_HEADS = 4
DIL_PATTERNS = ((128, 1), (512, 4), (2048, 16))
DIL_Q_HEADS = DIL_KV_HEADS * len(DIL_PATTERNS)
MOBA_HEADS = 6
MOBA_BLOCK = 256
MOBA_TOPK = 3
Q_BLOCK = 128
ROPE_THETA = 500000.0
ROPE_DIM = HEAD_DIM // 4
N_BRANCH = 3
D_FF = -(-(8 * D_MODEL) // (3 * 256)) * 256
LN_EPS = 1e-5
DN_ALPHA = (2 * DEPTH) ** 0.25
DN_BETA = (8 * DEPTH) ** -0.25
IN_SIZES = (FOX_HEADS * HEAD_DIM, FOX_HEADS * HEAD_DIM, FOX_HEADS * HEAD_DIM, FOX_HEADS,
            DIL_Q_HEADS * HEAD_DIM, DIL_KV_HEADS * HEAD_DIM, DIL_KV_HEADS * HEAD_DIM,
            MOBA_HEADS * HEAD_DIM, MOBA_HEADS * HEAD_DIM, MOBA_HEADS * HEAD_DIM,
            N_BRANCH * D_MODEL)
D_IN = sum(IN_SIZES)

kernel_name = 'hybrid_fox_dilated_moba_deepnorm'


def _layer_norm(x, g, b):
    xf = x.astype(jnp.float32)
    mu = jnp.mean(xf, axis=-1, keepdims=True)
    var = jnp.mean(jnp.square(xf - mu), axis=-1, keepdims=True)
    y = (xf - mu) * lax.rsqrt(var + LN_EPS)
    return (y * g + b).astype(x.dtype)


def _heads(t, n):
    b, s, _ = t.shape
    return t.reshape(b, s, n, HEAD_DIM).transpose(0, 2, 1, 3)


def _merge_heads(t):
    b, h, s, d = t.shape
    return t.transpose(0, 2, 1, 3).reshape(b, s, h * d)


def _rope(t, pos):
    half = ROPE_DIM // 2
    inv_freq = jnp.power(ROPE_THETA, -jnp.arange(half, dtype=jnp.float32) * 2.0 / ROPE_DIM)
    ang = pos.astype(jnp.float32)[:, None] * inv_freq[None, :]
    cos, sin = jnp.cos(ang), jnp.sin(ang)
    t1 = t[..., :half].astype(jnp.float32)
    t2 = t[..., half:ROPE_DIM].astype(jnp.float32)
    rot = jnp.concatenate([t1 * cos - t2 * sin, t2 * cos + t1 * sin], axis=-1).astype(t.dtype)
    return jnp.concatenate([rot, t[..., ROPE_DIM:]], axis=-1)


def _fox_attention(q, k, v, f_logit):
    b, h, s, d = q.shape
    scale = d ** -0.5
    logf = jax.nn.log_sigmoid(f_logit.astype(jnp.float32)).transpose(0, 2, 1)
    c = jnp.cumsum(logf, axis=-1)
    nq = s // Q_BLOCK
    qb = q.reshape(b, h, nq, Q_BLOCK, d).transpose(2, 0, 1, 3, 4)
    cb = c.reshape(b, h, nq, Q_BLOCK).transpose(2, 0, 1, 3)
    kpos = jnp.arange(s)

    def block(args):
        i, qi, ci = args
        logits = jnp.einsum('bhqd,bhkd->bhqk', qi, k, preferred_element_type=jnp.float32) * scale
        logits = logits + ci[..., None] - c[:, :, None, :]
        qpos = i * Q_BLOCK + jnp.arange(Q_BLOCK)
        logits = jnp.where(kpos[None, :] <= qpos[:, None], logits, -jnp.inf)
        p = jax.nn.softmax(logits, axis=-1)
        return jnp.einsum('bhqk,bhkd->bhqd', p.astype(v.dtype), v, preferred_element_type=jnp.float32)

    o = lax.map(block, (jnp.arange(nq), qb, cb))
    return o.transpose(1, 2, 0, 3, 4).reshape(b, h, s, d)


def _dilated_group(q, k, v, window, dil):
    b, h, s, d = q.shape
    scale = d ** -0.5
    span = window // dil
    n_prev = -(-span // Q_BLOCK)
    length = -(-s // (dil * Q_BLOCK)) * Q_BLOCK
    pad = length * dil - s
    nb = length // Q_BLOCK

    def strided(t):
        t = jnp.pad(t, ((0, 0), (0, 0), (0, pad), (0, 0)))
        t = t.reshape(b, h, length, dil, d).transpose(0, 1, 3, 2, 4)
        return t.reshape(b, h, dil, nb, Q_BLOCK, d)

    def band(t):
        tp = jnp.pad(t, ((0, 0), (0, 0), (0, 0), (n_prev, 0), (0, 0), (0, 0)))
        return jnp.concatenate([tp[:, :, :, j:j + nb] for j in range(n_prev + 1)], axis=4)

    qs = strided(q)
    kb, vb = band(strided(k)), band(strided(v))
    logits = jnp.einsum('bhrnqd,bhrnkd->bhrnqk', qs, kb, preferred_element_type=jnp.float32) * scale
    qi = jnp.arange(Q_BLOCK)[:, None] + n_prev * Q_BLOCK
    ki = jnp.arange((n_prev + 1) * Q_BLOCK)[None, :]
    dist = qi - ki
    abs_k = jnp.arange(nb)[:, None, None] * Q_BLOCK - n_prev * Q_BLOCK + ki[None]
    mask = (dist >= 0)[None] & (dist <= span)[None] & (abs_k >= 0)
    logits = jnp.where(mask, logits, -jnp.inf)
    m = jnp.max(logits, axis=-1, keepdims=True)
    p = jnp.exp(logits - m)
    den = jnp.sum(p, axis=-1, keepdims=True)
    o = jnp.einsum('bhrnqk,bhrnkd->bhrnqd', p.astype(v.dtype), vb, preferred_element_type=jnp.float32) / den
    lse = m[..., 0] + jnp.log(den[..., 0])
    o = o.reshape(b, h, dil, length, d).transpose(0, 1, 3, 2, 4).reshape(b, h, length * dil, d)[:, :, :s]
    lse = lse.reshape(b, h, dil, length).transpose(0, 1, 3, 2).reshape(b, h, length * dil)[:, :, :s]
    return o, lse


def _dilated_attention(q, k, v):
    outs, lses = [], []
    for g, (window, dil) in enumerate(DIL_PATTERNS):
        o, l = _dilated_group(q[:, g * DIL_KV_HEADS:(g + 1) * DIL_KV_HEADS], k, v, window, dil)
        outs.append(o)
        lses.append(l)
    w = jax.nn.softmax(jnp.stack(lses, axis=0), axis=0)
    return jnp.sum(w[..., None] * jnp.stack(outs, axis=0), axis=0)


def _moba_attention(q, k, v):
    b, h, s, d = q.shape
    scale = d ** -0.5
    nkb = -(-s // MOBA_BLOCK)
    sp = nkb * MOBA_BLOCK
    kp = jnp.pad(k, ((0, 0), (0, 0), (0, sp - s), (0, 0)))
    vp = jnp.pad(v, ((0, 0), (0, 0), (0, sp - s), (0, 0)))
    kb = kp.reshape(b, h, nkb, MOBA_BLOCK, d)
    vb = vp.reshape(b, h, nkb, MOBA_BLOCK, d)
    kmean = jnp.mean(kb.astype(jnp.float32), axis=3)
    topk = min(MOBA_TOPK, nkb)
    nq = s // Q_BLOCK
    qb = q.reshape(b, h, nq, Q_BLOCK, d).transpose(2, 0, 1, 3, 4)
    bi = jnp.arange(b)[:, None, None, None]
    hi = jnp.arange(h)[None, :, None, None]

    def block(args):
        i, qi = args
        qpos = i * Q_BLOCK + jnp.arange(Q_BLOCK)
        own_blk = (i * Q_BLOCK) // MOBA_BLOCK
        gate = jnp.einsum('bhqd,bhnd->bhqn', qi.astype(jnp.float32), kmean)
        past = jnp.arange(nkb)[None, :] < (qpos // MOBA_BLOCK)[:, None]
        gate = jnp.where(past, gate, -jnp.inf)
        gval, idx = lax.top_k(gate, topk)
        valid = jnp.isfinite(gval)
        ksel = kb[bi, hi, idx]
        vsel = vb[bi, hi, idx]
        s_sel = jnp.einsum('bhqd,bhqnkd->bhqnk', qi, ksel, preferred_element_type=jnp.float32) * scale
        s_sel = jnp.where(valid[..., None], s_sel, -jnp.inf).reshape(b, h, Q_BLOCK, topk * MOBA_BLOCK)
        kown = lax.dynamic_slice_in_dim(kp, own_blk * MOBA_BLOCK, MOBA_BLOCK, axis=2)
        vown = lax.dynamic_slice_in_dim(vp, own_blk * MOBA_BLOCK, MOBA_BLOCK, axis=2)
        s_own = jnp.einsum('bhqd,bhkd->bhqk', qi, kown, preferred_element_type=jnp.float32) * scale
        kpos_own = own_blk * MOBA_BLOCK + jnp.arange(MOBA_BLOCK)
        s_own = jnp.where(kpos_own[None, :] <= qpos[:, None], s_own, -jnp.inf)
        p = jax.nn.softmax(jnp.concatenate([s_sel, s_own], axis=-1), axis=-1)
        p_sel = p[..., :topk * MOBA_BLOCK].reshape(b, h, Q_BLOCK, topk, MOBA_BLOCK).astype(v.dtype)
        p_own = p[..., topk * MOBA_BLOCK:].astype(v.dtype)
        return (jnp.einsum('bhqnk,bhqnkd->bhqd', p_sel, vsel, preferred_element_type=jnp.float32)
                + jnp.einsum('bhqk,bhkd->bhqd', p_own, vown, preferred_element_type=jnp.float32))

    o = lax.map(block, (jnp.arange(nq), qb))
    return o.transpose(1, 2, 0, 3, 4).reshape(b, h, s, d)


def _mixer(x, w_in, b_f, w_pa, w_pb, w_pc, w_o):
    bsz, s, _ = x.shape
    pos = jnp.arange(s)
    hcat = jnp.einsum('bsd,de->bse', x, w_in)
    offsets = [int(o) for o in np.cumsum(IN_SIZES)[:-1]]
    qa, ka, va, fa, qb, kb, vb, qc, kc, vc, g = jnp.split(hcat, offsets, axis=-1)
    ya = _fox_attention(_heads(qa, FOX_HEADS), _heads(ka, FOX_HEADS), _heads(va, FOX_HEADS), fa + b_f)
    yb = _dilated_attention(_rope(_heads(qb, DIL_Q_HEADS), pos), _rope(_heads(kb, DIL_KV_HEADS), pos),
                            _heads(vb, DIL_KV_HEADS))
    yc = _moba_attention(_rope(_heads(qc, MOBA_HEADS), pos), _rope(_heads(kc, MOBA_HEADS), pos),
                         _heads(vc, MOBA_HEADS))
    gates = jax.nn.sigmoid(g.astype(jnp.float32)).reshape(bsz, s, N_BRANCH, D_MODEL)
    merged = (gates[:, :, 0] * jnp.einsum('bse,ed->bsd', _merge_heads(ya).astype(x.dtype), w_pa)
              + gates[:, :, 1] * jnp.einsum('bse,ed->bsd', _merge_heads(yb).astype(x.dtype), w_pb)
              + gates[:, :, 2] * jnp.einsum('bse,ed->bsd', _merge_heads(yc).astype(x.dtype), w_pc))
    return jnp.einsum('bsd,de->bse', merged.astype(x.dtype), w_o)


def _swiglu(x, w_gate, w_up, w_down):
    hid = jax.nn.silu(jnp.einsum('bsd,df->bsf', x, w_gate)) * jnp.einsum('bsd,df->bsf', x, w_up)
    return jnp.einsum('bsf,fd->bsd', hid, w_down)


def setup_inputs(seed: int = 0) -> dict:
    key = jax.random.key(seed)
    ks = jax.random.split(key, 20)
    f32 = jnp.float32
    in_scales = (1.0, 1.0, DN_BETA, 0.1, 1.0, 1.0, DN_BETA, 1.0, 1.0, DN_BETA, 1.0)
    col_scale = jnp.concatenate([jnp.full((n,), sc, f32) for n, sc in zip(IN_SIZES, in_scales)])
    wa = FOX_HEADS * HEAD_DIM
    wb = DIL_KV_HEADS * HEAD_DIM
    wc = MOBA_HEADS * HEAD_DIM
    nrm = lambda k, shp: jax.random.normal(k, shp, f32)
    return {
        'x': nrm(ks[0], (BATCH, SEQ, D_MODEL)),
        'ln_in_g': 1.0 + 0.02 * nrm(ks[1], (D_MODEL,)),
        'ln_in_b': 0.02 * nrm(ks[2], (D_MODEL,)),
        'w_in': nrm(ks[3], (DEPTH, D_MODEL, D_IN)) * D_MODEL ** -0.5 * col_scale,
        'b_f': jax.random.uniform(ks[4], (DEPTH, FOX_HEADS), f32, 2.0, 6.0),
        'w_pa': nrm(ks[5], (DEPTH, wa, D_MODEL)) * wa ** -0.5,
        'w_pb': nrm(ks[6], (DEPTH, wb, D_MODEL)) * wb ** -0.5,
        'w_pc': nrm(ks[7], (DEPTH, wc, D_MODEL)) * wc ** -0.5,
        'w_o': nrm(ks[8], (DEPTH, D_MODEL, D_MODEL)) * D_MODEL ** -0.5 * DN_BETA,
        'ln1_g': 1.0 + 0.02 * nrm(ks[9], (DEPTH, D_MODEL)),
        'ln1_b': 0.02 * nrm(ks[10], (DEPTH, D_MODEL)),
        'w_gate': nrm(ks[11], (DEPTH, D_MODEL, D_FF)) * D_MODEL ** -0.5,
        'w_up': nrm(ks[12], (DEPTH, D_MODEL, D_FF)) * D_MODEL ** -0.5,
        'w_down': nrm(ks[13], (DEPTH, D_FF, D_MODEL)) * D_FF ** -0.5 * DN_BETA,
        'ln2_g': 1.0 + 0.02 * nrm(ks[14], (DEPTH, D_MODEL)),
        'ln2_b': 0.02 * nrm(ks[15], (DEPTH, D_MODEL)),
    }


def reference(x, ln_in_g, ln_in_b, w_in, b_f, w_pa, w_pb, w_pc, w_o, ln1_g, ln1_b,
              w_gate, w_up, w_down, ln2_g, ln2_b):
    x = _layer_norm(x, ln_in_g, ln_in_b)
    for l in range(DEPTH):
        mix = _mixer(x, w_in[l], b_f[l], w_pa[l], w_pb[l], w_pc[l], w_o[l])
        x = _layer_norm(DN_ALPHA * x + mix, ln1_g[l], ln1_b[l])
        ffn = _swiglu(x, w_gate[l], w_up[l], w_down[l])
        x = _layer_norm(DN_ALPHA * x + ffn, ln2_g[l], ln2_b[l])
    return x
```

```python
import functools

import jax
import jax.numpy as jnp
import numpy as np
from jax import lax
from jax.experimental import pallas as pl
from jax.experimental.pallas import tpu as pltpu

D_MODEL = 1024
DEPTH = 2
HEAD_DIM = 64
FOX_HEADS = 6
DIL_KV_HEADS = 4
DIL_PATTERNS = ((128, 1), (512, 4), (2048, 16))
DIL_Q_HEADS = DIL_KV_HEADS * len(DIL_PATTERNS)
MOBA_HEADS = 6
MOBA_BLOCK = 256
MOBA_TOPK = 3
ROPE_THETA = 500000.0
ROPE_DIM = HEAD_DIM // 4
N_BRANCH = 3
D_FF = -(-(8 * D_MODEL) // (3 * 256)) * 256
LN_EPS = 1e-5
DN_ALPHA = (2 * DEPTH) ** 0.25
SCALE = HEAD_DIM ** -0.5

LANES = 128
HEADS_PER_BLOCK = LANES // HEAD_DIM
VMEM_LIMIT = 48 * 1024 * 1024

WA = FOX_HEADS * HEAD_DIM
WBQ = DIL_Q_HEADS * HEAD_DIM
WBK = DIL_KV_HEADS * HEAD_DIM
WC = MOBA_HEADS * HEAD_DIM
N_GATE = N_BRANCH * D_MODEL
SEG_FOX = (0, 3 * WA)
SEG_F = (SEG_FOX[1], SEG_FOX[1] + LANES)
SEG_DQ = (SEG_F[1], SEG_F[1] + WBQ)
SEG_DK = (SEG_DQ[1], SEG_DQ[1] + WBK)
SEG_DV = (SEG_DK[1], SEG_DK[1] + WBK)
SEG_MQ = (SEG_DV[1], SEG_DV[1] + WC)
SEG_MK = (SEG_MQ[1], SEG_MQ[1] + WC)
SEG_MV = (SEG_MK[1], SEG_MK[1] + WC)
N_PROJ = SEG_MV[1]

NEG_BIG = -1e30
BF16 = jnp.bfloat16
F32 = jnp.float32


def _cparams(sem):
    return pltpu.CompilerParams(dimension_semantics=sem, vmem_limit_bytes=VMEM_LIMIT)


def _dot(a, b):
    return jnp.dot(a, b, preferred_element_type=F32)


def _dot_nt(a, b):
    return lax.dot_general(a, b, (((1,), (1,)), ((), ())), preferred_element_type=F32)


def _layer_norm_rows(y, g, b):
    mu = jnp.mean(y, axis=-1, keepdims=True)
    yc = y - mu
    var = jnp.mean(yc * yc, axis=-1, keepdims=True)
    return yc * lax.rsqrt(var + LN_EPS) * g + b


def _ln_kernel(x_ref, g_ref, b_ref, o_ref):
    o_ref[...] = _layer_norm_rows(x_ref[...], g_ref[...], b_ref[...])


def _layer_norm(x2d, g, b, tm=512):
    m, d = x2d.shape
    return pl.pallas_call(
        _ln_kernel,
        out_shape=jax.ShapeDtypeStruct((m, d), F32),
        grid=(m // tm,),
        in_specs=[pl.BlockSpec((tm, d), lambda i: (i, 0)),
                  pl.BlockSpec((1, d), lambda i: (0, 0)),
                  pl.BlockSpec((1, d), lambda i: (0, 0))],
        out_specs=pl.BlockSpec((tm, d), lambda i: (i, 0)),
        compiler_params=_cparams(("parallel",)),
        name="ln_in",
    )(x2d, g.reshape(1, d), b.reshape(1, d))


def _rope_block(t, c, s1, s2):
    return t * c + pltpu.roll(t, LANES - ROPE_DIM // 2, 1) * s1 + pltpu.roll(t, ROPE_DIM // 2, 1) * s2


def _in_proj_kernel(x_ref, w_ref, bf_ref, rc_ref, rs1_ref, rs2_ref,
                    fox_ref, flog_ref, dq_ref, dkv_ref, mq_ref, mk_ref, mv_ref, km_ref,
                    *, tm, n_moba_blocks):
    xb = x_ref[...].astype(BF16)
    rc, rs1, rs2 = rc_ref[...], rs1_ref[...], rs2_ref[...]

    def proj(seg):
        return _dot(xb, w_ref[:, seg[0]:seg[1]])

    def rope(t):
        return jnp.concatenate(
            [_rope_block(t[:, c:c + LANES], rc, rs1, rs2) for c in range(0, t.shape[1], LANES)], axis=1)

    fox_ref[...] = proj(SEG_FOX).astype(BF16)
    z = proj(SEG_F) + bf_ref[...]
    flog_ref[...] = jnp.minimum(z, 0.0) - jnp.log1p(jnp.exp(-jnp.abs(z)))

    dq_ref[...] = rope(proj(SEG_DQ)).astype(BF16)
    dkv_ref[:, :WBK] = rope(proj(SEG_DK)).astype(BF16)
    dkv_ref[:, WBK:] = proj(SEG_DV).astype(BF16)

    mq_ref[...] = rope(proj(SEG_MQ)).astype(BF16)
    mv_ref[...] = proj(SEG_MV).astype(BF16)
    kc = rope(proj(SEG_MK))
    for r in range(tm // MOBA_BLOCK):
        km_ref[r] = jnp.sum(kc[r * MOBA_BLOCK:(r + 1) * MOBA_BLOCK], axis=0, keepdims=True) * (1.0 / MOBA_BLOCK)
    row = pl.program_id(0) * tm + lax.broadcasted_iota(jnp.int32, (tm, LANES), 0)
    blk = (row // MOBA_BLOCK) % n_moba_blocks
    lane = lax.broadcasted_iota(jnp.int32, (tm, LANES), 1)
    lo = lane < HEAD_DIM
    oh_hi = jnp.where(lane - HEAD_DIM == blk, 1.0, 0.0)
    oh_lo = jnp.where(lane == blk, 1.0, 0.0)
    for p in range(WC // LANES):
        kp = kc[:, p * LANES:(p + 1) * LANES]
        mk_ref[:, (2 * p) * LANES:(2 * p + 1) * LANES] = jnp.where(lo, kp, oh_hi).astype(BF16)
        mk_ref[:, (2 * p + 1) * LANES:(2 * p + 2) * LANES] = jnp.where(lo, oh_lo, kp).astype(BF16)


def _in_proj(x2d, w_cat, bf_pad, rope_tabs, seq, tm=512):
    m, d = x2d.shape
    n_rt = seq // tm
    n_moba_blocks = seq // MOBA_BLOCK
    assert n_moba_blocks <= HEAD_DIM and seq % tm == 0 and tm % MOBA_BLOCK == 0
    row = lambda i: (i, 0)
    tab = lambda i: (i % n_rt, 0)
    outs = (
        jax.ShapeDtypeStruct((m, 3 * WA), BF16),
        jax.ShapeDtypeStruct((m, LANES), F32),
        jax.ShapeDtypeStruct((m, WBQ), BF16),
        jax.ShapeDtypeStruct((m, 2 * WBK), BF16),
        jax.ShapeDtypeStruct((m, WC), BF16),
        jax.ShapeDtypeStruct((m, 2 * WC), BF16),
        jax.ShapeDtypeStruct((m, WC), BF16),
        jax.ShapeDtypeStruct((m // MOBA_BLOCK, 1, WC), F32),
    )
    return pl.pallas_call(
        functools.partial(_in_proj_kernel, tm=tm, n_moba_blocks=n_moba_blocks),
        out_shape=outs,
        grid=(m // tm,),
        in_specs=[pl.BlockSpec((tm, d), row),
                  pl.BlockSpec((d, N_PROJ), lambda i: (0, 0)),
                  pl.BlockSpec((1, LANES), lambda i: (0, 0)),
                  pl.BlockSpec((tm, LANES), tab),
                  pl.BlockSpec((tm, LANES), tab),
                  pl.BlockSpec((tm, LANES), tab)],
        out_specs=(pl.BlockSpec((tm, 3 * WA), row),
                   pl.BlockSpec((tm, LANES), row),
                   pl.BlockSpec((tm, WBQ), row),
                   pl.BlockSpec((tm, 2 * WBK), row),
                   pl.BlockSpec((tm, WC), row),
                   pl.BlockSpec((tm, 2 * WC), row),
                   pl.BlockSpec((tm, WC), row),
                   pl.BlockSpec((tm // MOBA_BLOCK, 1, WC), lambda i: (i, 0, 0))),
        compiler_params=_cparams(("parallel",)),
        name="in_proj",
    )(x2d, w_cat, bf_pad, *rope_tabs)


def _cumsum_kernel(f_ref, c_ref, ct_ref, carry_ref, *, tc):
    @pl.when(pl.program_id(1) == 0)
    def _():
        carry_ref[...] = jnp.zeros_like(carry_ref)

    f = f_ref[0]
    r = lax.broadcasted_iota(jnp.int32, (tc, tc), 0)
    c = lax.broadcasted_iota(jnp.int32, (tc, tc), 1)
    tri = jnp.where(c <= r, 1.0, 0.0).astype(BF16)
    f_hi = f.astype(BF16)
    r1 = f - f_hi.astype(F32)
    f_mid = r1.astype(BF16)
    f_lo = (r1 - f_mid.astype(F32)).astype(BF16)
    cs = _dot(tri, f_hi) + _dot(tri, f_mid) + _dot(tri, f_lo) + carry_ref[...]
    c_ref[0] = cs
    carry_ref[...] = cs[tc - 1:tc, :]
    ct_ref[0, 0] = cs.T[:8, :]


def _cumsum(flog, tc):
    b, s, _ = flog.shape
    return pl.pallas_call(
        functools.partial(_cumsum_kernel, tc=tc),
        out_shape=(jax.ShapeDtypeStruct((b, s, LANES), F32),
                   jax.ShapeDtypeStruct((b, s // tc, 8, tc), F32)),
        grid=(b, s // tc),
        in_specs=[pl.BlockSpec((1, tc, LANES), lambda bi, i: (bi, i, 0))],
        out_specs=(pl.BlockSpec((1, tc, LANES), lambda bi, i: (bi, i, 0)),
                   pl.BlockSpec((1, 1, 8, tc), lambda bi, i: (bi, i, 0, 0))),
        scratch_shapes=[pltpu.VMEM((1, LANES), F32)],
        compiler_params=_cparams(("parallel", "arbitrary")),
        name="fox_cumsum",
    )(flog)


def _softmax_step(s, v, m_ref, l_ref, acc_ref):
    m_old = m_ref[...]
    m_new = jnp.maximum(m_old, jnp.max(s, axis=1, keepdims=True))
    alpha = jnp.exp(m_old - m_new)
    p = jnp.exp(s - m_new)
    l_ref[...] = alpha * l_ref[...] + jnp.sum(p, axis=1, keepdims=True)
    acc_ref[...] = alpha * acc_ref[...] + _dot(p.astype(BF16), v)
    m_ref[...] = m_new


def _init_stats(m_refs, l_refs, acc_refs):
    for m_ref, l_ref, acc_ref in zip(m_refs, l_refs, acc_refs):
        m_ref[...] = jnp.full_like(m_ref, NEG_BIG)
        l_ref[...] = jnp.zeros_like(l_ref)
        acc_ref[...] = jnp.zeros_like(acc_ref)


def _head_masks(shape):
    lane = lax.broadcasted_iota(jnp.int32, shape, 1)
    return [lane // HEAD_DIM == h for h in range(HEADS_PER_BLOCK)]


def _fox_kernel(q_ref, k_ref, v_ref, cq_ref, ck_ref, o_ref,
                m0, m1, l0, l1, a0, a1, *, t):
    pair = pl.program_id(1)
    i = pl.program_id(2)
    q = q_ref[0]
    hm = _head_masks((t, LANES))
    qh = [jnp.where(hm[h], q, jnp.zeros_like(q)) for h in range(HEADS_PER_BLOCK)]
    cq_all = cq_ref[0]
    lane = lax.broadcasted_iota(jnp.int32, (t, LANES), 1)
    cq = [jnp.sum(jnp.where(lane == HEADS_PER_BLOCK * pair + h, cq_all, 0.0), axis=1, keepdims=True)
          for h in range(HEADS_PER_BLOCK)]
    ms, ls, accs = (m0, m1), (l0, l1), (a0, a1)
    _init_stats(ms, ls, accs)

    def chunk(j, masked):
        start = pl.multiple_of(j * t, t)
        k = k_ref[0, pl.ds(start, t), :]
        v = v_ref[0, pl.ds(start, t), :]
        ck_all = ck_ref[0, j]
        sub = lax.broadcasted_iota(jnp.int32, (8, t), 0)
        for h in range(HEADS_PER_BLOCK):
            ck = jnp.sum(jnp.where(sub == HEADS_PER_BLOCK * pair + h, ck_all, 0.0), axis=0, keepdims=True)
            s = _dot_nt(qh[h], k) + cq[h] - ck
            if masked:
                r = lax.broadcasted_iota(jnp.int32, (t, t), 0)
                c = lax.broadcasted_iota(jnp.int32, (t, t), 1)
                s = jnp.where(c <= r, s, -jnp.inf)
            _softmax_step(s, v, ms[h], ls[h], accs[h])

    chunk(i, True)

    def body(j, carry):
        chunk(j, False)
        return carry

    lax.fori_loop(0, i, body, 0)
    o_ref[0] = jnp.where(hm[0], a0[...] / l0[...], a1[...] / l1[...]).astype(o_ref.dtype)


def _fox_attention(fox, c, ct, t):
    b, s, _ = fox.shape
    n_pairs = FOX_HEADS // HEADS_PER_BLOCK
    stat = pltpu.VMEM((t, 1), F32)
    acc = pltpu.VMEM((t, LANES), F32)
    return pl.pallas_call(
        functools.partial(_fox_kernel, t=t),
        out_shape=jax.ShapeDtypeStruct((b, s, WA), BF16),
        grid=(b, n_pairs, s // t),
        in_specs=[pl.BlockSpec((1, t, LANES), lambda bi, p, i: (bi, i, p)),
                  pl.BlockSpec((1, s, LANES), lambda bi, p, i: (bi, 0, n_pairs + p)),
                  pl.BlockSpec((1, s, LANES), lambda bi, p, i: (bi, 0, 2 * n_pairs + p)),
                  pl.BlockSpec((1, t, LANES), lambda bi, p, i: (bi, i, 0)),
                  pl.BlockSpec((1, s // t, 8, t), lambda bi, p, i: (bi, 0, 0, 0))],
        out_specs=pl.BlockSpec((1, t, LANES), lambda bi, p, i: (bi, i, p)),
        scratch_shapes=[stat, stat, stat, stat, acc, acc],
        compiler_params=_cparams(("parallel", "parallel", "arbitrary")),
        name="fox_attn",
    )(fox, fox, fox, c, ct)


def _dil_kernel(q_ref, kc_ref, kp_ref, vc_ref, vp_ref, o_ref, lse_ref, *, t, span, prev):
    i = pl.program_id(3)
    q = q_ref[0]
    kc, kp, vc, vp = kc_ref[0], kp_ref[0], vc_ref[0], vp_ref[0]
    hm = _head_masks((t, LANES))
    r_c = lax.broadcasted_iota(jnp.int32, (t, t), 0)
    c_c = lax.broadcasted_iota(jnp.int32, (t, t), 1)
    dist_c = r_c - c_c
    mask_c = (dist_c >= 0) & (dist_c <= span)
    r_p = lax.broadcasted_iota(jnp.int32, (t, prev), 0)
    c_p = lax.broadcasted_iota(jnp.int32, (t, prev), 1)
    dist_p = r_p - c_p + prev + jnp.where(i > 0, 0, 2 * span + 2)
    mask_p = dist_p <= span
    outs, lses = [], []
    for h in range(HEADS_PER_BLOCK):
        qh = jnp.where(hm[h], q, jnp.zeros_like(q))
        s_c = jnp.where(mask_c, _dot_nt(qh, kc), -jnp.inf)
        s_p = jnp.where(mask_p, _dot_nt(qh, kp), -jnp.inf)
        m = jnp.maximum(jnp.max(s_c, axis=1, keepdims=True), jnp.max(s_p, axis=1, keepdims=True))
        p_c = jnp.exp(s_c - m)
        p_p = jnp.exp(s_p - m)
        den = jnp.sum(p_c, axis=1, keepdims=True) + jnp.sum(p_p, axis=1, keepdims=True)
        o = (_dot(p_c.astype(BF16), vc) + _dot(p_p.astype(BF16), vp)) / den
        outs.append(o)
        lses.append(m + jnp.log(den))
    o_ref[0] = jnp.where(hm[0], outs[0], outs[1])
    lse_ref[0] = jnp.where(hm[0], lses[0], lses[1])


def _dilated_group(dq, dkv, group, window, dil, t):
    b, s, _ = dq.shape
    span = window // dil
    prev = LANES
    assert span <= prev and s % (dil * t) == 0 and t % prev == 0
    n_kv_pairs = DIL_KV_HEADS // HEADS_PER_BLOCK
    qb_per_row = WBQ // LANES
    kvb_per_row = 2 * WBK // LANES
    ob_per_row = WBK // LANES
    sd = s // dil
    dq_v = dq.reshape(b, sd, dil * WBQ)
    dkv_v = dkv.reshape(b, sd, dil * 2 * WBK)
    tp = t // prev
    cur = lambda col: (lambda bi, r, p, i: (bi, i, col(r, p)))
    prv = lambda col: (lambda bi, r, p, i: (bi, jnp.maximum(i * tp - 1, 0), col(r, p)))
    qcol = lambda r, p: r * qb_per_row + group * n_kv_pairs + p
    kcol = lambda r, p: r * kvb_per_row + p
    vcol = lambda r, p: r * kvb_per_row + n_kv_pairs + p
    ocol = lambda r, p: r * ob_per_row + p
    o, lse = pl.pallas_call(
        functools.partial(_dil_kernel, t=t, span=span, prev=prev),
        out_shape=(jax.ShapeDtypeStruct((b, sd, dil * WBK), F32),
                   jax.ShapeDtypeStruct((b, sd, dil * WBK), F32)),
        grid=(b, dil, n_kv_pairs, sd // t),
        in_specs=[pl.BlockSpec((1, t, LANES), cur(qcol)),
                  pl.BlockSpec((1, t, LANES), cur(kcol)),
                  pl.BlockSpec((1, prev, LANES), prv(kcol)),
                  pl.BlockSpec((1, t, LANES), cur(vcol)),
                  pl.BlockSpec((1, prev, LANES), prv(vcol))],
        out_specs=(pl.BlockSpec((1, t, LANES), cur(ocol)),
                   pl.BlockSpec((1, t, LANES), cur(ocol))),
        compiler_params=_cparams(("parallel", "parallel", "parallel", "parallel")),
        name=f"dilated_attn_g{group}",
    )(dq_v, dkv_v, dkv_v, dkv_v, dkv_v)
    return o.reshape(b * s, WBK), lse.reshape(b * s, WBK)


def _moba_select_bias(gate, n_past, lane_off):
    lane = lax.broadcasted_iota(jnp.int32, gate.shape, 1)
    blk = lane - lane_off
    g = jnp.where((blk >= 0) & (blk < n_past), gate, -jnp.inf)
    sel = jnp.zeros(gate.shape, jnp.bool_)
    for _ in range(MOBA_TOPK):
        mx = jnp.max(g, axis=1, keepdims=True)
        is_max = (g == mx) & (mx > -jnp.inf)
        idx = jnp.min(jnp.where(is_max, lane, 2 * LANES), axis=1, keepdims=True)
        pick = lane == idx
        sel = sel | pick
        g = jnp.where(pick, -jnp.inf, g)
    return jnp.where(sel, 0.0, NEG_BIG)


def _moba_kernel(q_ref, k0_ref, k1_ref, v_ref, km_ref, o_ref,
                 m0, m1, l0, l1, a0, a1, *, t):
    i = pl.program_id(2)
    q = q_ref[0]
    hm = _head_masks((t, LANES))
    km = km_ref[0]
    nb = km.shape[0]
    k_refs = (k0_ref, k1_ref)
    ms, ls, accs = (m0, m1), (l0, l1), (a0, a1)
    _init_stats(ms, ls, accs)
    qaug = []
    for h in range(HEADS_PER_BLOCK):
        qh = jnp.where(hm[h], q, jnp.zeros_like(q))
        lane_off = HEAD_DIM * (1 - h)
        pads = (lane_off, LANES - lane_off - nb)
        kmat = jnp.concatenate([jnp.zeros((pads[0], LANES), F32)] * (pads[0] > 0) + [km]
                               + [jnp.zeros((pads[1], LANES), F32)] * (pads[1] > 0), axis=0)
        k_hi = kmat.astype(BF16)
        r1 = kmat - k_hi.astype(F32)
        k_mid = r1.astype(BF16)
        k_lo = (r1 - k_mid.astype(F32)).astype(BF16)
        gate = _dot_nt(qh, k_hi) + _dot_nt(qh, k_mid) + _dot_nt(qh, k_lo)
        bias = _moba_select_bias(gate, i, lane_off)
        qaug.append(jnp.where(hm[h], q, bias.astype(BF16)))

    start_own = pl.multiple_of(i * t, t)
    v_own = v_ref[0, pl.ds(start_own, t), :]
    r = lax.broadcasted_iota(jnp.int32, (t, t), 0)
    c = lax.broadcasted_iota(jnp.int32, (t, t), 1)
    for h in range(HEADS_PER_BLOCK):
        qh = jnp.where(hm[h], q, jnp.zeros_like(q))
        s = jnp.where(c <= r, _dot_nt(qh, k_refs[h][0, pl.ds(start_own, t), :]), -jnp.inf)
        _softmax_step(s, v_own, ms[h], ls[h], accs[h])

    def body(j, carry):
        start = pl.multiple_of(j * t, t)
        v = v_ref[0, pl.ds(start, t), :]
        for h in range(HEADS_PER_BLOCK):
            s = _dot_nt(qaug[h], k_refs[h][0, pl.ds(start, t), :])
            _softmax_step(s, v, ms[h], ls[h], accs[h])
        return carry

    lax.fori_loop(0, i, body, 0)
    o_ref[0] = jnp.where(hm[0], a0[...] / l0[...], a1[...] / l1[...]).astype(o_ref.dtype)


def _moba_attention(mq, mk, mv, km, t=MOBA_BLOCK):
    b, s, _ = mq.shape
    n_pairs = MOBA_HEADS // HEADS_PER_BLOCK
    nb = s // MOBA_BLOCK
    stat = pltpu.VMEM((t, 1), F32)
    acc = pltpu.VMEM((t, LANES), F32)
    return pl.pallas_call(
        functools.partial(_moba_kernel, t=t),
        out_shape=jax.ShapeDtypeStruct((b, s, WC), BF16),
        grid=(b, n_pairs, s // t),
        in_specs=[pl.BlockSpec((1, t, LANES), lambda bi, p, i: (bi, i, p)),
                  pl.BlockSpec((1, s, LANES), lambda bi, p, i: (bi, 0, 2 * p)),
                  pl.BlockSpec((1, s, LANES), lambda bi, p, i: (bi, 0, 2 * p + 1)),
                  pl.BlockSpec((1, s, LANES), lambda bi, p, i: (bi, 0, p)),
                  pl.BlockSpec((1, nb, LANES), lambda bi, p, i: (bi, 0, p))],
        out_specs=pl.BlockSpec((1, t, LANES), lambda bi, p, i: (bi, i, p)),
        scratch_shapes=[stat, stat, stat, stat, acc, acc],
        compiler_params=_cparams(("parallel", "parallel", "arbitrary")),
        name="moba_attn",
    )(mq, mk, mk, mv, km)


def _sigmoid(z):
    return 1.0 / (1.0 + jnp.exp(-z))


def _merge_kernel(x_ref, ya_ref, yc_ref, o0_ref, o1_ref, o2_ref, e0_ref, e1_ref, e2_ref,
                  wg_ref, wpa_ref, wpb_ref, wpc_ref, wo_ref, g_ref, b_ref, out_ref):
    x = x_ref[...]
    xb = x.astype(BF16)
    e0, e1, e2 = e0_ref[...], e1_ref[...], e2_ref[...]
    mx = jnp.maximum(jnp.maximum(e0, e1), e2)
    w0, w1, w2 = jnp.exp(e0 - mx), jnp.exp(e1 - mx), jnp.exp(e2 - mx)
    yb = (w0 * o0_ref[...] + w1 * o1_ref[...] + w2 * o2_ref[...]) / (w0 + w1 + w2)
    merged = (_sigmoid(_dot(xb, wg_ref[:, :D_MODEL])) * _dot(ya_ref[...], wpa_ref[...])
              + _sigmoid(_dot(xb, wg_ref[:, D_MODEL:2 * D_MODEL])) * _dot(yb.astype(BF16), wpb_ref[...])
              + _sigmoid(_dot(xb, wg_ref[:, 2 * D_MODEL:])) * _dot(yc_ref[...], wpc_ref[...]))
    mix = _dot(merged.astype(BF16), wo_ref[...])
    out_ref[...] = _layer_norm_rows(DN_ALPHA * x + mix, g_ref[...], b_ref[...])


def _merge(x2d, ya, yc, dil_o, dil_lse, wg, wpa, wpb, wpc, wo, g, b, tm=256):
    m, d = x2d.shape
    row = lambda i: (i, 0)
    full = lambda i: (0, 0)
    rows = lambda w: pl.BlockSpec((tm, w), row)
    whole = lambda a: pl.BlockSpec(a.shape, full)
    return pl.pallas_call(
        _merge_kernel,
        out_shape=jax.ShapeDtypeStruct((m, d), F32),
        grid=(m // tm,),
        in_specs=[rows(d), rows(WA), rows(WC)] + [rows(WBK)] * 6
                 + [whole(wg), whole(wpa), whole(wpb), whole(wpc), whole(wo),
                    pl.BlockSpec((1, d), full), pl.BlockSpec((1, d), full)],
        out_specs=rows(d),
        compiler_params=_cparams(("parallel",)),
        name="merge_proj_ln",
    )(x2d, ya, yc, *dil_o, *dil_lse, wg, wpa, wpb, wpc, wo, g.reshape(1, d), b.reshape(1, d))


def _ffn_kernel(x_ref, wg_ref, wu_ref, wd_ref, g_ref, b_ref, out_ref, acc_ref):
    f = pl.program_id(1)
    x = x_ref[...]
    xb = x.astype(BF16)
    hg = _dot(xb, wg_ref[...])
    hid = hg * _sigmoid(hg) * _dot(xb, wu_ref[...])
    part = _dot(hid.astype(BF16), wd_ref[...])

    @pl.when(f == 0)
    def _():
        acc_ref[...] = part

    @pl.when(f > 0)
    def _():
        acc_ref[...] += part

    @pl.when(f == pl.num_programs(1) - 1)
    def _():
        out_ref[...] = _layer_norm_rows(DN_ALPHA * x + acc_ref[...], g_ref[...], b_ref[...])


def _ffn(x2d, wg, wu, wd, g, b, tm=512, n_f=2):
    m, d = x2d.shape
    tf = D_FF // n_f
    assert tf % LANES == 0
    return pl.pallas_call(
        _ffn_kernel,
        out_shape=jax.ShapeDtypeStruct((m, d), F32),
        grid=(m // tm, n_f),
        in_specs=[pl.BlockSpec((tm, d), lambda i, f: (i, 0)),
                  pl.BlockSpec((d, tf), lambda i, f: (0, f)),
                  pl.BlockSpec((d, tf), lambda i, f: (0, f)),
                  pl.BlockSpec((tf, d), lambda i, f: (f, 0)),
                  pl.BlockSpec((1, d), lambda i, f: (0, 0)),
                  pl.BlockSpec((1, d), lambda i, f: (0, 0))],
        out_specs=pl.BlockSpec((tm, d), lambda i, f: (i, 0)),
        scratch_shapes=[pltpu.VMEM((tm, d), F32)],
        compiler_params=_cparams(("parallel", "arbitrary")),
        name="swiglu_ln",
    )(x2d, wg, wu, wd, g.reshape(1, d), b.reshape(1, d))


def _rope_tables(seq):
    half = ROPE_DIM // 2
    inv_freq = jnp.power(ROPE_THETA, -jnp.arange(half, dtype=F32) * 2.0 / ROPE_DIM)
    ang = jnp.arange(seq).astype(F32)[:, None] * inv_freq[None, :]
    cos, sin = jnp.cos(ang), jnp.sin(ang)
    ones = jnp.ones((seq, HEAD_DIM - ROPE_DIM), F32)
    zeros = jnp.zeros((seq, HEAD_DIM - ROPE_DIM), F32)
    zh = jnp.zeros((seq, half), F32)
    c = jnp.concatenate([cos, cos, ones], axis=1)
    s1 = jnp.concatenate([-sin, zh, zeros], axis=1)
    s2 = jnp.concatenate([zh, sin, zeros], axis=1)
    tile = lambda a: jnp.concatenate([a] * HEADS_PER_BLOCK, axis=1)
    return tile(c), tile(s1), tile(s2)


def _pack_w_in(w, b_f):
    sizes = (WA, WA, WA, FOX_HEADS, WBQ, WBK, WBK, WC, WC, WC, N_GATE)
    offs = np.cumsum(sizes)[:-1].tolist()
    qa, ka, va, fa, qb, kb, vb, qc, kc, vc, g = jnp.split(w, offs, axis=1)
    fa_pad = jnp.pad(fa, ((0, 0), (0, LANES - FOX_HEADS)))
    w_cat = jnp.concatenate([qa * SCALE, ka, va, fa_pad, qb * SCALE, kb, vb, qc * SCALE, kc, vc], axis=1)
    bf_pad = jnp.pad(b_f, (0, LANES - FOX_HEADS)).reshape(1, LANES)
    return w_cat.astype(BF16), g.astype(BF16), bf_pad


def kernel(x, ln_in_g, ln_in_b, w_in, b_f, w_pa, w_pb, w_pc, w_o, ln1_g, ln1_b,
           w_gate, w_up, w_down, ln2_g, ln2_b):
    bsz, seq, d = x.shape
    m = bsz * seq
    t_attn = MOBA_BLOCK
    rope_tabs = _rope_tables(seq)
    h = _layer_norm(x.reshape(m, d), ln_in_g, ln_in_b)
    for l in range(DEPTH):
        w_cat, w_g, bf_pad = _pack_w_in(w_in[l], b_f[l])
        fox, flog, dq, dkv, mq, mk, mv, km = _in_proj(h, w_cat, bf_pad, rope_tabs, seq)
        shp = lambda a: a.reshape(bsz, seq, a.shape[-1])
        c, ct = _cumsum(shp(flog), t_attn)
        ya = _fox_attention(shp(fox), c, ct, t_attn)
        dil = [_dilated_group(shp(dq), shp(dkv), g, window, dil_, t_attn)
               for g, (window, dil_) in enumerate(DIL_PATTERNS)]
        yc = _moba_attention(shp(mq), shp(mk), shp(mv), km.reshape(bsz, seq // MOBA_BLOCK, WC))
        h = _merge(h, ya.reshape(m, WA), yc.reshape(m, WC), [o for o, _ in dil], [e for _, e in dil],
                   w_g, w_pa[l].astype(BF16), w_pb[l].astype(BF16), w_pc[l].astype(BF16),
                   w_o[l].astype(BF16), ln1_g[l], ln1_b[l])
        h = _ffn(h, w_gate[l].astype(BF16), w_up[l].astype(BF16), w_down[l].astype(BF16),
                 ln2_g[l], ln2_b[l])
    return h.reshape(bsz, seq, d)
```

```python
import functools

import jax
import jax.numpy as jnp
import numpy as np
from jax import lax
from jax.experimental import pallas as pl
from jax.experimental.pallas import tpu as pltpu

D_MODEL = 1024
DEPTH = 2
HEAD_DIM = 64
FOX_HEADS = 6
DIL_KV_HEADS = 4
DIL_PATTERNS = ((128, 1), (512, 4), (2048, 16))
DIL_Q_HEADS = DIL_KV_HEADS * len(DIL_PATTERNS)
MOBA_HEADS = 6
MOBA_BLOCK = 256
MOBA_TOPK = 3
ROPE_THETA = 500000.0
ROPE_DIM = HEAD_DIM // 4
N_BRANCH = 3
D_FF = -(-(8 * D_MODEL) // (3 * 256)) * 256
LN_EPS = 1e-5
DN_ALPHA = (2 * DEPTH) ** 0.25
SCALE = HEAD_DIM ** -0.5

LANES = 128
HEADS_PER_BLOCK = LANES // HEAD_DIM
VMEM_LIMIT = 48 * 1024 * 1024

WA = FOX_HEADS * HEAD_DIM
WBQ = DIL_Q_HEADS * HEAD_DIM
WBK = DIL_KV_HEADS * HEAD_DIM
WC = MOBA_HEADS * HEAD_DIM
N_GATE = N_BRANCH * D_MODEL
SEG_FOX = (0, 3 * WA)
SEG_F = (SEG_FOX[1], SEG_FOX[1] + LANES)
SEG_DQ = (SEG_F[1], SEG_F[1] + WBQ)
SEG_DK = (SEG_DQ[1], SEG_DQ[1] + WBK)
SEG_DV = (SEG_DK[1], SEG_DK[1] + WBK)
SEG_MQ = (SEG_DV[1], SEG_DV[1] + WC)
SEG_MK = (SEG_MQ[1], SEG_MQ[1] + WC)
SEG_MV = (SEG_MK[1], SEG_MK[1] + WC)
N_PROJ = SEG_MV[1]

T_ATTN = 2 * MOBA_BLOCK
T_DIL = 256
NEG_BIG = -1e30
BF16 = jnp.bfloat16
F32 = jnp.float32


def _cparams(sem):
    return pltpu.CompilerParams(dimension_semantics=sem, vmem_limit_bytes=VMEM_LIMIT)


def _dot(a, b):
    return jnp.dot(a, b, preferred_element_type=F32)


def _dot_nt(a, b):
    return lax.dot_general(a, b, (((1,), (1,)), ((), ())), preferred_element_type=F32)


def _split3(x):
    hi = x.astype(BF16)
    r1 = x - hi.astype(F32)
    mid = r1.astype(BF16)
    lo = (r1 - mid.astype(F32)).astype(BF16)
    return hi, mid, lo


def _layer_norm_rows(y, g, b):
    mu = jnp.mean(y, axis=-1, keepdims=True)
    yc = y - mu
    var = jnp.mean(yc * yc, axis=-1, keepdims=True)
    return yc * lax.rsqrt(var + LN_EPS) * g + b


def _ln_kernel(x_ref, g_ref, b_ref, o_ref):
    o_ref[...] = _layer_norm_rows(x_ref[...], g_ref[...], b_ref[...])


def _layer_norm(x2d, g, b, tm=512):
    m, d = x2d.shape
    return pl.pallas_call(
        _ln_kernel,
        out_shape=jax.ShapeDtypeStruct((m, d), F32),
        grid=(m // tm,),
        in_specs=[pl.BlockSpec((tm, d), lambda i: (i, 0)),
                  pl.BlockSpec((1, d), lambda i: (0, 0)),
                  pl.BlockSpec((1, d), lambda i: (0, 0))],
        out_specs=pl.BlockSpec((tm, d), lambda i: (i, 0)),
        compiler_params=_cparams(("parallel",)),
        name="ln_in",
    )(x2d, g.reshape(1, d), b.reshape(1, d))


def _rope_block(t, c, s1, s2):
    return t * c + pltpu.roll(t, LANES - ROPE_DIM // 2, 1) * s1 + pltpu.roll(t, ROPE_DIM // 2, 1) * s2


def _in_proj_kernel(x_ref, w_ref, bf_ref, rc_ref, rs1_ref, rs2_ref,
                    fqk_ref, fvt_ref, flog_ref, dq_ref, dkv_ref, mq_ref, mk_ref, mvt_ref, km_ref,
                    *, tm, tk, n_moba_blocks):
    xb = x_ref[...].astype(BF16)
    rc, rs1, rs2 = rc_ref[...], rs1_ref[...], rs2_ref[...]

    def proj(seg):
        return _dot(xb, w_ref[:, seg[0]:seg[1]])

    def rope(t):
        return jnp.concatenate(
            [_rope_block(t[:, c:c + LANES], rc, rs1, rs2) for c in range(0, t.shape[1], LANES)], axis=1)

    def store_transposed(vt_ref, v):
        for p in range(v.shape[1] // LANES):
            vt = v[:, p * LANES:(p + 1) * LANES].T
            for c in range(tm // tk):
                vt_ref[0, p, c] = vt[:, c * tk:(c + 1) * tk].astype(BF16)

    fox = proj(SEG_FOX)
    fqk_ref[...] = fox[:, :2 * WA].astype(BF16)
    store_transposed(fvt_ref, fox[:, 2 * WA:])
    z = proj(SEG_F) + bf_ref[...]
    flog_ref[...] = jnp.minimum(z, 0.0) - jnp.log1p(jnp.exp(-jnp.abs(z)))

    dq_ref[...] = rope(proj(SEG_DQ)).astype(BF16)
    dkv_ref[:, :WBK] = rope(proj(SEG_DK)).astype(BF16)
    dkv_ref[:, WBK:] = proj(SEG_DV).astype(BF16)

    mq_ref[...] = rope(proj(SEG_MQ)).astype(BF16)
    store_transposed(mvt_ref, proj(SEG_MV))
    kc = rope(proj(SEG_MK))
    for r in range(tm // MOBA_BLOCK):
        km_ref[r] = jnp.sum(kc[r * MOBA_BLOCK:(r + 1) * MOBA_BLOCK], axis=0, keepdims=True) * (1.0 / MOBA_BLOCK)
    row = pl.program_id(0) * tm + lax.broadcasted_iota(jnp.int32, (tm, LANES), 0)
    blk = (row // MOBA_BLOCK) % n_moba_blocks
    lane = lax.broadcasted_iota(jnp.int32, (tm, LANES), 1)
    lo = lane < HEAD_DIM
    oh_hi = jnp.where(lane - HEAD_DIM == blk, 1.0, 0.0)
    oh_lo = jnp.where(lane == blk, 1.0, 0.0)
    for p in range(WC // LANES):
        kp = kc[:, p * LANES:(p + 1) * LANES]
        mk_ref[:, (2 * p) * LANES:(2 * p + 1) * LANES] = jnp.where(lo, kp, oh_hi).astype(BF16)
        mk_ref[:, (2 * p + 1) * LANES:(2 * p + 2) * LANES] = jnp.where(lo, oh_lo, kp).astype(BF16)


def _in_proj(x2d, w_cat, bf_pad, rope_tabs, bsz, seq, tm=512, tk=T_ATTN):
    m, d = x2d.shape
    n_rt = seq // tm
    n_moba_blocks = seq // MOBA_BLOCK
    assert n_moba_blocks <= HEAD_DIM and seq % tm == 0 and tm % MOBA_BLOCK == 0 and tm % tk == 0
    row = lambda i: (i, 0)
    tab = lambda i: (i % n_rt, 0)
    vt_shape = lambda w: jax.ShapeDtypeStruct((bsz, w // LANES, seq // tk, LANES, tk), BF16)
    vt_spec = lambda w: pl.BlockSpec((1, w // LANES, tm // tk, LANES, tk),
                                     lambda i: (i // n_rt, 0, i % n_rt, 0, 0))
    outs = (
        jax.ShapeDtypeStruct((m, 2 * WA), BF16),
        vt_shape(WA),
        jax.ShapeDtypeStruct((m, LANES), F32),
        jax.ShapeDtypeStruct((m, WBQ), BF16),
        jax.ShapeDtypeStruct((m, 2 * WBK), BF16),
        jax.ShapeDtypeStruct((m, WC), BF16),
        jax.ShapeDtypeStruct((m, 2 * WC), BF16),
        vt_shape(WC),
        jax.ShapeDtypeStruct((m // MOBA_BLOCK, 1, WC), F32),
    )
    return pl.pallas_call(
        functools.partial(_in_proj_kernel, tm=tm, tk=tk, n_moba_blocks=n_moba_blocks),
        out_shape=outs,
        grid=(m // tm,),
        in_specs=[pl.BlockSpec((tm, d), row),
                  pl.BlockSpec((d, N_PROJ), lambda i: (0, 0)),
                  pl.BlockSpec((1, LANES), lambda i: (0, 0)),
                  pl.BlockSpec((tm, LANES), tab),
                  pl.BlockSpec((tm, LANES), tab),
                  pl.BlockSpec((tm, LANES), tab)],
        out_specs=(pl.BlockSpec((tm, 2 * WA), row),
                   vt_spec(WA),
                   pl.BlockSpec((tm, LANES), row),
                   pl.BlockSpec((tm, WBQ), row),
                   pl.BlockSpec((tm, 2 * WBK), row),
                   pl.BlockSpec((tm, WC), row),
                   pl.BlockSpec((tm, 2 * WC), row),
                   vt_spec(WC),
                   pl.BlockSpec((tm // MOBA_BLOCK, 1, WC), lambda i: (i, 0, 0))),
        compiler_params=_cparams(("parallel",)),
        name="in_proj",
    )(x2d, w_cat, bf_pad, *rope_tabs)


def _fox_pack_kernel(f_ref, q_ref, k_ref, qa_ref, ka_ref, carry_ref, *, tc):
    @pl.when(pl.program_id(1) == 0)
    def _():
        carry_ref[...] = jnp.zeros_like(carry_ref)

    f = f_ref[0]
    r = lax.broadcasted_iota(jnp.int32, (tc, tc), 0)
    c = lax.broadcasted_iota(jnp.int32, (tc, tc), 1)
    tri = jnp.where(c <= r, 1.0, 0.0).astype(BF16)
    f_hi, f_mid, f_lo = _split3(f)
    cs = _dot(tri, f_hi) + _dot(tri, f_mid) + _dot(tri, f_lo) + carry_ref[...]
    carry_ref[...] = cs[tc - 1:tc, :]

    lane = lax.broadcasted_iota(jnp.int32, (tc, LANES), 1)
    q, k = q_ref[0], k_ref[0]
    for h in range(FOX_HEADS):
        col = jnp.sum(jnp.where(lane == h, cs, 0.0), axis=1, keepdims=True)
        hi, mid, lo = [piece.astype(F32) for piece in _split3(col)]
        a = lane - HEAD_DIM * (1 - h % HEADS_PER_BLOCK)
        one = jnp.where((a >= 3) & (a < 6), 1.0, 0.0)
        aux_q = jnp.where(a == 0, hi, jnp.where(a == 1, mid, jnp.where(a == 2, lo, one)))
        one = jnp.where((a >= 0) & (a < 3), 1.0, 0.0)
        aux_k = jnp.where(a == 3, -hi, jnp.where(a == 4, -mid, jnp.where(a == 5, -lo, one)))
        own = lane // HEAD_DIM == h % HEADS_PER_BLOCK
        p = h // HEADS_PER_BLOCK
        qa_ref[0, :, h * LANES:(h + 1) * LANES] = jnp.where(own, q[:, p * LANES:(p + 1) * LANES], aux_q.astype(BF16))
        ka_ref[0, :, h * LANES:(h + 1) * LANES] = jnp.where(own, k[:, p * LANES:(p + 1) * LANES], aux_k.astype(BF16))


def _fox_pack(flog, fqk, tc=512):
    b, s, _ = flog.shape
    n_pairs = WA // LANES
    aug = jax.ShapeDtypeStruct((b, s, FOX_HEADS * LANES), BF16)
    return pl.pallas_call(
        functools.partial(_fox_pack_kernel, tc=tc),
        out_shape=(aug, aug),
        grid=(b, s // tc),
        in_specs=[pl.BlockSpec((1, tc, LANES), lambda bi, i: (bi, i, 0)),
                  pl.BlockSpec((1, tc, WA), lambda bi, i: (bi, i, 0)),
                  pl.BlockSpec((1, tc, WA), lambda bi, i: (bi, i, 1))],
        out_specs=(pl.BlockSpec((1, tc, FOX_HEADS * LANES), lambda bi, i: (bi, i, 0)),
                   pl.BlockSpec((1, tc, FOX_HEADS * LANES), lambda bi, i: (bi, i, 0))),
        scratch_shapes=[pltpu.VMEM((1, LANES), F32)],
        compiler_params=_cparams(("parallel", "arbitrary")),
        name="fox_pack",
    )(flog, fqk, fqk)


def _attend(qs, k_refs, vt_ref, acc_refs, i, t):
    key = lax.broadcasted_iota(jnp.int32, (t, t), 0)
    qry = lax.broadcasted_iota(jnp.int32, (t, t), 1)
    causal = key <= qry
    start = pl.multiple_of(i * t, t)
    vt = vt_ref[0, 0, i]
    stats = []
    for h in range(HEADS_PER_BLOCK):
        s = jnp.where(causal, _dot_nt(k_refs[h][0, pl.ds(start, t), :], qs[h]), -jnp.inf)
        m = jnp.max(s, axis=0, keepdims=True)
        p = jnp.exp(s - m)
        stats += [m, jnp.sum(p, axis=0, keepdims=True)]
        acc_refs[h][...] = _dot(vt[h * HEAD_DIM:(h + 1) * HEAD_DIM], p.astype(BF16))

    def body(j, carry):
        st = pl.multiple_of(j * t, t)
        vt = vt_ref[0, 0, j]
        out = []
        ss = [_dot_nt(k_refs[h][0, pl.ds(st, t), :], qs[h]) for h in range(HEADS_PER_BLOCK)]
        ps, alphas = [], []
        for h in range(HEADS_PER_BLOCK):
            m, l = carry[2 * h], carry[2 * h + 1]
            m_new = jnp.maximum(m, jnp.max(ss[h], axis=0, keepdims=True))
            alpha = jnp.exp(m - m_new)
            p = jnp.exp(ss[h] - m_new)
            out += [m_new, alpha * l + jnp.sum(p, axis=0, keepdims=True)]
            ps.append(p.astype(BF16))
            alphas.append(alpha)
        pv = [_dot(vt[h * HEAD_DIM:(h + 1) * HEAD_DIM], ps[h]) for h in range(HEADS_PER_BLOCK)]
        for h in range(HEADS_PER_BLOCK):
            acc_refs[h][...] = alphas[h] * acc_refs[h][...] + pv[h]
        return tuple(out)

    stats = lax.fori_loop(0, i, body, tuple(stats))
    o_t = jnp.concatenate([acc_refs[h][...] / stats[2 * h + 1] for h in range(HEADS_PER_BLOCK)], axis=0)
    return o_t.T


def _attn_scratch(t):
    return [pltpu.VMEM((HEAD_DIM, t), F32)] * HEADS_PER_BLOCK


def _head_masks(shape):
    lane = lax.broadcasted_iota(jnp.int32, shape, 1)
    return [lane // HEAD_DIM == h for h in range(HEADS_PER_BLOCK)]


def _fox_kernel(q0_ref, q1_ref, k0_ref, k1_ref, vt_ref, o_ref, a0, a1, *, t):
    o = _attend((q0_ref[0], q1_ref[0]), (k0_ref, k1_ref), vt_ref, (a0, a1), pl.program_id(2), t)
    o_ref[0] = o.astype(o_ref.dtype)


def _fox_attention(qa, ka, vt, t=T_ATTN):
    b, s, _ = qa.shape
    n_pairs = FOX_HEADS // HEADS_PER_BLOCK
    qspec = lambda h: pl.BlockSpec((1, t, LANES), lambda bi, p, i: (bi, i, HEADS_PER_BLOCK * p + h))
    kspec = lambda h: pl.BlockSpec((1, s, LANES), lambda bi, p, i: (bi, 0, HEADS_PER_BLOCK * p + h))
    return pl.pallas_call(
        functools.partial(_fox_kernel, t=t),
        out_shape=jax.ShapeDtypeStruct((b, s, WA), BF16),
        grid=(b, n_pairs, s // t),
        in_specs=[qspec(0), qspec(1), kspec(0), kspec(1),
                  pl.BlockSpec((1, 1, s // t, LANES, t), lambda bi, p, i: (bi, p, 0, 0, 0))],
        out_specs=pl.BlockSpec((1, t, LANES), lambda bi, p, i: (bi, i, p)),
        scratch_shapes=_attn_scratch(t),
        compiler_params=_cparams(("parallel", "parallel", "arbitrary")),
        name="fox_attn",
    )(qa, qa, ka, ka, vt)


def _dil_kernel(q_ref, kc_ref, kp_ref, vc_ref, vp_ref, o_ref, lse_ref, *, t, span, prev):
    i = pl.program_id(3)
    q = q_ref[0]
    kc, kp, vc, vp = kc_ref[0], kp_ref[0], vc_ref[0], vp_ref[0]
    hm = _head_masks((t, LANES))
    r_c = lax.broadcasted_iota(jnp.int32, (t, t), 0)
    c_c = lax.broadcasted_iota(jnp.int32, (t, t), 1)
    dist_c = r_c - c_c
    mask_c = (dist_c >= 0) & (dist_c <= span)
    r_p = lax.broadcasted_iota(jnp.int32, (t, prev), 0)
    c_p = lax.broadcasted_iota(jnp.int32, (t, prev), 1)
    dist_p = r_p - c_p + prev + jnp.where(i > 0, 0, 2 * span + 2)
    mask_p = dist_p <= span
    outs, lses = [], []
    for h in range(HEADS_PER_BLOCK):
        qh = jnp.where(hm[h], q, jnp.zeros_like(q))
        s_c = jnp.where(mask_c, _dot_nt(qh, kc), -jnp.inf)
        s_p = jnp.where(mask_p, _dot_nt(qh, kp), -jnp.inf)
        m = jnp.maximum(jnp.max(s_c, axis=1, keepdims=True), jnp.max(s_p, axis=1, keepdims=True))
        p_c = jnp.exp(s_c - m)
        p_p = jnp.exp(s_p - m)
        den = jnp.sum(p_c, axis=1, keepdims=True) + jnp.sum(p_p, axis=1, keepdims=True)
        o = (_dot(p_c.astype(BF16), vc) + _dot(p_p.astype(BF16), vp)) / den
        outs.append(o)
        lses.append(m + jnp.log(den))
    o_ref[0] = jnp.where(hm[0], outs[0], outs[1])
    lse_ref[0] = jnp.where(hm[0], lses[0], lses[1])


def _dilated_group(dq, dkv, group, window, dil, t):
    b, s, _ = dq.shape
    span = window // dil
    prev = LANES
    assert span <= prev and s % (dil * t) == 0 and t % prev == 0
    n_kv_pairs = DIL_KV_HEADS // HEADS_PER_BLOCK
    qb_per_row = WBQ // LANES
    kvb_per_row = 2 * WBK // LANES
    ob_per_row = WBK // LANES
    sd = s // dil
    dq_v = dq.reshape(b, sd, dil * WBQ)
    dkv_v = dkv.reshape(b, sd, dil * 2 * WBK)
    tp = t // prev
    cur = lambda col: (lambda bi, r, p, i: (bi, i, col(r, p)))
    prv = lambda col: (lambda bi, r, p, i: (bi, jnp.maximum(i * tp - 1, 0), col(r, p)))
    qcol = lambda r, p: r * qb_per_row + group * n_kv_pairs + p
    kcol = lambda r, p: r * kvb_per_row + p
    vcol = lambda r, p: r * kvb_per_row + n_kv_pairs + p
    ocol = lambda r, p: r * ob_per_row + p
    o, lse = pl.pallas_call(
        functools.partial(_dil_kernel, t=t, span=span, prev=prev),
        out_shape=(jax.ShapeDtypeStruct((b, sd, dil * WBK), F32),
                   jax.ShapeDtypeStruct((b, sd, dil * WBK), F32)),
        grid=(b, dil, n_kv_pairs, sd // t),
        in_specs=[pl.BlockSpec((1, t, LANES), cur(qcol)),
                  pl.BlockSpec((1, t, LANES), cur(kcol)),
                  pl.BlockSpec((1, prev, LANES), prv(kcol)),
                  pl.BlockSpec((1, t, LANES), cur(vcol)),
                  pl.BlockSpec((1, prev, LANES), prv(vcol))],
        out_specs=(pl.BlockSpec((1, t, LANES), cur(ocol)),
                   pl.BlockSpec((1, t, LANES), cur(ocol))),
        compiler_params=_cparams(("parallel", "parallel", "parallel", "parallel")),
        name=f"dilated_attn_g{group}",
    )(dq_v, dkv_v, dkv_v, dkv_v, dkv_v)
    return o.reshape(b * s, WBK), lse.reshape(b * s, WBK)


def _moba_select_bias(gate, n_past, lane_off):
    lane = lax.broadcasted_iota(jnp.int32, gate.shape, 1)
    blk = lane - lane_off
    g = jnp.where((blk >= 0) & (blk < n_past), gate, -jnp.inf)
    sel = blk == n_past
    for _ in range(MOBA_TOPK):
        mx = jnp.max(g, axis=1, keepdims=True)
        is_max = (g == mx) & (mx > -jnp.inf)
        idx = jnp.min(jnp.where(is_max, lane, 2 * LANES), axis=1, keepdims=True)
        pick = lane == idx
        sel = sel | pick
        g = jnp.where(pick, -jnp.inf, g)
    return jnp.where(sel, 0.0, NEG_BIG)


def _moba_kernel(q_ref, k0_ref, k1_ref, vt_ref, km_ref, o_ref, a0, a1, *, t):
    i = pl.program_id(2)
    q = q_ref[0]
    hm = _head_masks((t, LANES))
    km = km_ref[0]
    nb = km.shape[0]
    own_blk = (i * t + lax.broadcasted_iota(jnp.int32, (t, 1), 0)) // MOBA_BLOCK
    qaug = []
    for h in range(HEADS_PER_BLOCK):
        qh = jnp.where(hm[h], q, jnp.zeros_like(q))
        lane_off = HEAD_DIM * (1 - h)
        pads = (lane_off, LANES - lane_off - nb)
        kmat = jnp.concatenate([jnp.zeros((pads[0], LANES), F32)] * (pads[0] > 0) + [km]
                               + [jnp.zeros((pads[1], LANES), F32)] * (pads[1] > 0), axis=0)
        k_hi, k_mid, k_lo = _split3(kmat)
        gate = _dot_nt(qh, k_hi) + _dot_nt(qh, k_mid) + _dot_nt(qh, k_lo)
        bias = _moba_select_bias(gate, own_blk, lane_off)
        qaug.append(jnp.where(hm[h], q, bias.astype(BF16)))
    o = _attend(qaug, (k0_ref, k1_ref), vt_ref, (a0, a1), i, t)
    o_ref[0] = o.astype(o_ref.dtype)


def _moba_attention(mq, mk, mvt, km, t=T_ATTN):
    b, s, _ = mq.shape
    assert t % MOBA_BLOCK == 0
    n_pairs = MOBA_HEADS // HEADS_PER_BLOCK
    nb = s // MOBA_BLOCK
    kspec = lambda h: pl.BlockSpec((1, s, LANES), lambda bi, p, i: (bi, 0, HEADS_PER_BLOCK * p + h))
    return pl.pallas_call(
        functools.partial(_moba_kernel, t=t),
        out_shape=jax.ShapeDtypeStruct((b, s, WC), BF16),
        grid=(b, n_pairs, s // t),
        in_specs=[pl.BlockSpec((1, t, LANES), lambda bi, p, i: (bi, i, p)),
                  kspec(0), kspec(1),
                  pl.BlockSpec((1, 1, s // t, LANES, t), lambda bi, p, i: (bi, p, 0, 0, 0)),
                  pl.BlockSpec((1, nb, LANES), lambda bi, p, i: (bi, 0, p))],
        out_specs=pl.BlockSpec((1, t, LANES), lambda bi, p, i: (bi, i, p)),
        scratch_shapes=_attn_scratch(t),
        compiler_params=_cparams(("parallel", "parallel", "arbitrary")),
        name="moba_attn",
    )(mq, mk, mk, mvt, km)


def _sigmoid(z):
    return 1.0 / (1.0 + jnp.exp(-z))


def _merge_kernel(x_ref, ya_ref, yc_ref, o0_ref, o1_ref, o2_ref, e0_ref, e1_ref, e2_ref,
                  wg_ref, wpa_ref, wpb_ref, wpc_ref, wo_ref, g_ref, b_ref, out_ref):
    x = x_ref[...]
    xb = x.astype(BF16)
    e0, e1, e2 = e0_ref[...], e1_ref[...], e2_ref[...]
    mx = jnp.maximum(jnp.maximum(e0, e1), e2)
    w0, w1, w2 = jnp.exp(e0 - mx), jnp.exp(e1 - mx), jnp.exp(e2 - mx)
    yb = (w0 * o0_ref[...] + w1 * o1_ref[...] + w2 * o2_ref[...]) / (w0 + w1 + w2)
    merged = (_sigmoid(_dot(xb, wg_ref[:, :D_MODEL])) * _dot(ya_ref[...], wpa_ref[...])
              + _sigmoid(_dot(xb, wg_ref[:, D_MODEL:2 * D_MODEL])) * _dot(yb.astype(BF16), wpb_ref[...])
              + _sigmoid(_dot(xb, wg_ref[:, 2 * D_MODEL:])) * _dot(yc_ref[...], wpc_ref[...]))
    mix = _dot(merged.astype(BF16), wo_ref[...])
    out_ref[...] = _layer_norm_rows(DN_ALPHA * x + mix, g_ref[...], b_ref[...])


def _merge(x2d, ya, yc, dil_o, dil_lse, wg, wpa, wpb, wpc, wo, g, b, tm=256):
    m, d = x2d.shape
    row = lambda i: (i, 0)
    full = lambda i: (0, 0)
    rows = lambda w: pl.BlockSpec((tm, w), row)
    whole = lambda a: pl.BlockSpec(a.shape, full)
    return pl.pallas_call(
        _merge_kernel,
        out_shape=jax.ShapeDtypeStruct((m, d), F32),
        grid=(m // tm,),
        in_specs=[rows(d), rows(WA), rows(WC)] + [rows(WBK)] * 6
                 + [whole(wg), whole(wpa), whole(wpb), whole(wpc), whole(wo),
                    pl.BlockSpec((1, d), full), pl.BlockSpec((1, d), full)],
        out_specs=rows(d),
        compiler_params=_cparams(("parallel",)),
        name="merge_proj_ln",
    )(x2d, ya, yc, *dil_o, *dil_lse, wg, wpa, wpb, wpc, wo, g.reshape(1, d), b.reshape(1, d))


def _ffn_kernel(x_ref, wg_ref, wu_ref, wd_ref, g_ref, b_ref, out_ref, acc_ref):
    f = pl.program_id(1)
    x = x_ref[...]
    xb = x.astype(BF16)
    hg = _dot(xb, wg_ref[...])
    hid = hg * _sigmoid(hg) * _dot(xb, wu_ref[...])
    part = _dot(hid.astype(BF16), wd_ref[...])

    @pl.when(f == 0)
    def _():
        acc_ref[...] = part

    @pl.when(f > 0)
    def _():
        acc_ref[...] += part

    @pl.when(f == pl.num_programs(1) - 1)
    def _():
        out_ref[...] = _layer_norm_rows(DN_ALPHA * x + acc_ref[...], g_ref[...], b_ref[...])


def _ffn(x2d, wg, wu, wd, g, b, tm=512, n_f=2):
    m, d = x2d.shape
    tf = D_FF // n_f
    assert tf % LANES == 0
    return pl.pallas_call(
        _ffn_kernel,
        out_shape=jax.ShapeDtypeStruct((m, d), F32),
        grid=(m // tm, n_f),
        in_specs=[pl.BlockSpec((tm, d), lambda i, f: (i, 0)),
                  pl.BlockSpec((d, tf), lambda i, f: (0, f)),
                  pl.BlockSpec((d, tf), lambda i, f: (0, f)),
                  pl.BlockSpec((tf, d), lambda i, f: (f, 0)),
                  pl.BlockSpec((1, d), lambda i, f: (0, 0)),
                  pl.BlockSpec((1, d), lambda i, f: (0, 0))],
        out_specs=pl.BlockSpec((tm, d), lambda i, f: (i, 0)),
        scratch_shapes=[pltpu.VMEM((tm, d), F32)],
        compiler_params=_cparams(("parallel", "arbitrary")),
        name="swiglu_ln",
    )(x2d, wg, wu, wd, g.reshape(1, d), b.reshape(1, d))


def _rope_tables(seq):
    half = ROPE_DIM // 2
    inv_freq = jnp.power(ROPE_THETA, -jnp.arange(half, dtype=F32) * 2.0 / ROPE_DIM)
    ang = jnp.arange(seq).astype(F32)[:, None] * inv_freq[None, :]
    cos, sin = jnp.cos(ang), jnp.sin(ang)
    ones = jnp.ones((seq, HEAD_DIM - ROPE_DIM), F32)
    zeros = jnp.zeros((seq, HEAD_DIM - ROPE_DIM), F32)
    zh = jnp.zeros((seq, half), F32)
    c = jnp.concatenate([cos, cos, ones], axis=1)
    s1 = jnp.concatenate([-sin, zh, zeros], axis=1)
    s2 = jnp.concatenate([zh, sin, zeros], axis=1)
    tile = lambda a: jnp.concatenate([a] * HEADS_PER_BLOCK, axis=1)
    return tile(c), tile(s1), tile(s2)


def _pack_w_in(w, b_f):
    sizes = (WA, WA, WA, FOX_HEADS, WBQ, WBK, WBK, WC, WC, WC, N_GATE)
    offs = np.cumsum(sizes)[:-1].tolist()
    qa, ka, va, fa, qb, kb, vb, qc, kc, vc, g = jnp.split(w, offs, axis=1)
    fa_pad = jnp.pad(fa, ((0, 0), (0, LANES - FOX_HEADS)))
    w_cat = jnp.concatenate([qa * SCALE, ka, va, fa_pad, qb * SCALE, kb, vb, qc * SCALE, kc, vc], axis=1)
    bf_pad = jnp.pad(b_f, (0, LANES - FOX_HEADS)).reshape(1, LANES)
    return w_cat.astype(BF16), g.astype(BF16), bf_pad


def kernel(x, ln_in_g, ln_in_b, w_in, b_f, w_pa, w_pb, w_pc, w_o, ln1_g, ln1_b,
           w_gate, w_up, w_down, ln2_g, ln2_b):
    bsz, seq, d = x.shape
    m = bsz * seq
    rope_tabs = _rope_tables(seq)
    h = _layer_norm(x.reshape(m, d), ln_in_g, ln_in_b)
    for l in range(DEPTH):
        w_cat, w_g, bf_pad = _pack_w_in(w_in[l], b_f[l])
        fqk, fvt, flog, dq, dkv, mq, mk, mvt, km = _in_proj(h, w_cat, bf_pad, rope_tabs, bsz, seq)
        shp = lambda a: a.reshape(bsz, seq, a.shape[-1])
        qa, ka = _fox_pack(shp(flog), shp(fqk))
        ya = _fox_attention(qa, ka, fvt)
        dil = [_dilated_group(shp(dq), shp(dkv), g, window, dil_, T_DIL)
               for g, (window, dil_) in enumerate(DIL_PATTERNS)]
        yc = _moba_attention(shp(mq), shp(mk), mvt, km.reshape(bsz, seq // MOBA_BLOCK, WC))
        h = _merge(h, ya.reshape(m, WA), yc.reshape(m, WC), [o for o, _ in dil], [e for _, e in dil],
                   w_g, w_pa[l].astype(BF16), w_pb[l].astype(BF16), w_pc[l].astype(BF16),
                   w_o[l].astype(BF16), ln1_g[l], ln1_b[l])
        h = _ffn(h, w_gate[l].astype(BF16), w_up[l].astype(BF16), w_down[l].astype(BF16),
                 ln2_g[l], ln2_b[l])
    return h.reshape(bsz, seq, d)
```

```python
import functools

import jax
import jax.numpy as jnp
import numpy as np
from jax import lax
from jax.experimental import pallas as pl
from jax.experimental.pallas import tpu as pltpu

D_MODEL = 1024
DEPTH = 2
HEAD_DIM = 64
FOX_HEADS = 6
DIL_KV_HEADS = 4
DIL_PATTERNS = ((128, 1), (512, 4), (2048, 16))
DIL_Q_HEADS = DIL_KV_HEADS * len(DIL_PATTERNS)
MOBA_HEADS = 6
MOBA_BLOCK = 256
MOBA_TOPK = 3
ROPE_THETA = 500000.0
ROPE_DIM = HEAD_DIM // 4
N_BRANCH = 3
D_FF = -(-(8 * D_MODEL) // (3 * 256)) * 256
LN_EPS = 1e-5
DN_ALPHA = (2 * DEPTH) ** 0.25
SCALE = HEAD_DIM ** -0.5

LANES = 128
HEADS_PER_BLOCK = LANES // HEAD_DIM
VMEM_LIMIT = 48 * 1024 * 1024

WA = FOX_HEADS * HEAD_DIM
WBQ = DIL_Q_HEADS * HEAD_DIM
WBK = DIL_KV_HEADS * HEAD_DIM
WC = MOBA_HEADS * HEAD_DIM
N_GATE = N_BRANCH * D_MODEL
SEG_FOX = (0, 3 * WA)
SEG_F = (SEG_FOX[1], SEG_FOX[1] + LANES)
SEG_DQ = (SEG_F[1], SEG_F[1] + WBQ)
SEG_DK = (SEG_DQ[1], SEG_DQ[1] + WBK)
SEG_DV = (SEG_DK[1], SEG_DK[1] + WBK)
SEG_MQ = (SEG_DV[1], SEG_DV[1] + WC)
SEG_MK = (SEG_MQ[1], SEG_MQ[1] + WC)
SEG_MV = (SEG_MK[1], SEG_MK[1] + WC)
N_PROJ = SEG_MV[1]

T_ATTN = 2 * MOBA_BLOCK
T_DIL = 256
NEG_BIG = -1e30
LOG2E = 1.4426950408889634
EXP2_HEADROOM = 64.0
CHUNKS_PER_STEP = 4
SUM_ROWS = 16
BF16 = jnp.bfloat16
F32 = jnp.float32


def _cparams(sem):
    return pltpu.CompilerParams(dimension_semantics=sem, vmem_limit_bytes=VMEM_LIMIT)


def _dot(a, b):
    return jnp.dot(a, b, preferred_element_type=F32)


def _dot_nt(a, b):
    return lax.dot_general(a, b, (((1,), (1,)), ((), ())), preferred_element_type=F32)


def _split3(x):
    hi = x.astype(BF16)
    r1 = x - hi.astype(F32)
    mid = r1.astype(BF16)
    lo = (r1 - mid.astype(F32)).astype(BF16)
    return hi, mid, lo


def _layer_norm_rows(y, g, b):
    mu = jnp.mean(y, axis=-1, keepdims=True)
    yc = y - mu
    var = jnp.mean(yc * yc, axis=-1, keepdims=True)
    return yc * lax.rsqrt(var + LN_EPS) * g + b


def _ln_kernel(x_ref, g_ref, b_ref, o_ref):
    o_ref[...] = _layer_norm_rows(x_ref[...], g_ref[...], b_ref[...])


def _layer_norm(x2d, g, b, tm=512):
    m, d = x2d.shape
    return pl.pallas_call(
        _ln_kernel,
        out_shape=jax.ShapeDtypeStruct((m, d), F32),
        grid=(m // tm,),
        in_specs=[pl.BlockSpec((tm, d), lambda i: (i, 0)),
                  pl.BlockSpec((1, d), lambda i: (0, 0)),
                  pl.BlockSpec((1, d), lambda i: (0, 0))],
        out_specs=pl.BlockSpec((tm, d), lambda i: (i, 0)),
        compiler_params=_cparams(("parallel",)),
        name="ln_in",
    )(x2d, g.reshape(1, d), b.reshape(1, d))


def _rope_block(t, c, s1, s2):
    return t * c + pltpu.roll(t, LANES - ROPE_DIM // 2, 1) * s1 + pltpu.roll(t, ROPE_DIM // 2, 1) * s2


def _in_proj_kernel(x_ref, w_ref, bf_ref, rc_ref, rs1_ref, rs2_ref,
                    fqk_ref, fvt_ref, flog_ref, dq_ref, dkv_ref, mq_ref, mk_ref, mvt_ref, km_ref,
                    *, tm, tk, n_moba_blocks):
    xb = x_ref[...].astype(BF16)
    rc, rs1, rs2 = rc_ref[...], rs1_ref[...], rs2_ref[...]

    def proj(seg):
        return _dot(xb, w_ref[:, seg[0]:seg[1]])

    def rope(t):
        return jnp.concatenate(
            [_rope_block(t[:, c:c + LANES], rc, rs1, rs2) for c in range(0, t.shape[1], LANES)], axis=1)

    def store_transposed(vt_ref, v):
        for p in range(v.shape[1] // LANES):
            vt = v[:, p * LANES:(p + 1) * LANES].T
            for c in range(tm // tk):
                vt_ref[0, p, c] = vt[:, c * tk:(c + 1) * tk].astype(BF16)

    fox = proj(SEG_FOX)
    fqk_ref[:, :WA] = (fox[:, :WA] * LOG2E).astype(BF16)
    fqk_ref[:, WA:] = fox[:, WA:2 * WA].astype(BF16)
    store_transposed(fvt_ref, fox[:, 2 * WA:])
    z = proj(SEG_F) + bf_ref[...]
    flog_ref[...] = jnp.minimum(z, 0.0) - jnp.log1p(jnp.exp(-jnp.abs(z)))

    dq_ref[...] = rope(proj(SEG_DQ)).astype(BF16)
    dkv_ref[:, :WBK] = rope(proj(SEG_DK)).astype(BF16)
    dkv_ref[:, WBK:] = proj(SEG_DV).astype(BF16)

    mq_ref[...] = (rope(proj(SEG_MQ)) * LOG2E).astype(BF16)
    store_transposed(mvt_ref, proj(SEG_MV))
    kc = rope(proj(SEG_MK))
    for r in range(tm // MOBA_BLOCK):
        km_ref[r] = jnp.sum(kc[r * MOBA_BLOCK:(r + 1) * MOBA_BLOCK], axis=0, keepdims=True) * (1.0 / MOBA_BLOCK)
    row = pl.program_id(0) * tm + lax.broadcasted_iota(jnp.int32, (tm, LANES), 0)
    blk = (row // MOBA_BLOCK) % n_moba_blocks
    lane = lax.broadcasted_iota(jnp.int32, (tm, LANES), 1)
    lo = lane < HEAD_DIM
    oh_hi = jnp.where(lane - HEAD_DIM == blk, 1.0, 0.0)
    oh_lo = jnp.where(lane == blk, 1.0, 0.0)
    for p in range(WC // LANES):
        kp = kc[:, p * LANES:(p + 1) * LANES]
        mk_ref[:, (2 * p) * LANES:(2 * p + 1) * LANES] = jnp.where(lo, kp, oh_hi).astype(BF16)
        mk_ref[:, (2 * p + 1) * LANES:(2 * p + 2) * LANES] = jnp.where(lo, oh_lo, kp).astype(BF16)


def _in_proj(x2d, w_cat, bf_pad, rope_tabs, bsz, seq, tm=512, tk=T_ATTN):
    m, d = x2d.shape
    n_rt = seq // tm
    n_moba_blocks = seq // MOBA_BLOCK
    assert n_moba_blocks <= HEAD_DIM and seq % tm == 0 and tm % MOBA_BLOCK == 0 and tm % tk == 0
    row = lambda i: (i, 0)
    tab = lambda i: (i % n_rt, 0)
    vt_shape = lambda w: jax.ShapeDtypeStruct((bsz, w // LANES, seq // tk, LANES, tk), BF16)
    vt_spec = lambda w: pl.BlockSpec((1, w // LANES, tm // tk, LANES, tk),
                                     lambda i: (i // n_rt, 0, i % n_rt, 0, 0))
    outs = (
        jax.ShapeDtypeStruct((m, 2 * WA), BF16),
        vt_shape(WA),
        jax.ShapeDtypeStruct((m, LANES), F32),
        jax.ShapeDtypeStruct((m, WBQ), BF16),
        jax.ShapeDtypeStruct((m, 2 * WBK), BF16),
        jax.ShapeDtypeStruct((m, WC), BF16),
        jax.ShapeDtypeStruct((m, 2 * WC), BF16),
        vt_shape(WC),
        jax.ShapeDtypeStruct((m // MOBA_BLOCK, 1, WC), F32),
    )
    return pl.pallas_call(
        functools.partial(_in_proj_kernel, tm=tm, tk=tk, n_moba_blocks=n_moba_blocks),
        out_shape=outs,
        grid=(m // tm,),
        in_specs=[pl.BlockSpec((tm, d), row),
                  pl.BlockSpec((d, N_PROJ), lambda i: (0, 0)),
                  pl.BlockSpec((1, LANES), lambda i: (0, 0)),
                  pl.BlockSpec((tm, LANES), tab),
                  pl.BlockSpec((tm, LANES), tab),
                  pl.BlockSpec((tm, LANES), tab)],
        out_specs=(pl.BlockSpec((tm, 2 * WA), row),
                   vt_spec(WA),
                   pl.BlockSpec((tm, LANES), row),
                   pl.BlockSpec((tm, WBQ), row),
                   pl.BlockSpec((tm, 2 * WBK), row),
                   pl.BlockSpec((tm, WC), row),
                   pl.BlockSpec((tm, 2 * WC), row),
                   vt_spec(WC),
                   pl.BlockSpec((tm // MOBA_BLOCK, 1, WC), lambda i: (i, 0, 0))),
        compiler_params=_cparams(("parallel",)),
        name="in_proj",
    )(x2d, w_cat, bf_pad, *rope_tabs)


def _fox_pack_kernel(f_ref, q_ref, k_ref, qa_ref, ka_ref, carry_ref, *, tc):
    @pl.when(pl.program_id(1) == 0)
    def _():
        carry_ref[...] = jnp.zeros_like(carry_ref)

    f = f_ref[0]
    r = lax.broadcasted_iota(jnp.int32, (tc, tc), 0)
    c = lax.broadcasted_iota(jnp.int32, (tc, tc), 1)
    tri = jnp.where(c <= r, 1.0, 0.0).astype(BF16)
    f_hi, f_mid, f_lo = _split3(f)
    cs = _dot(tri, f_hi) + _dot(tri, f_mid) + _dot(tri, f_lo) + carry_ref[...]
    carry_ref[...] = cs[tc - 1:tc, :]

    lane = lax.broadcasted_iota(jnp.int32, (tc, LANES), 1)
    q, k = q_ref[0], k_ref[0]
    for h in range(FOX_HEADS):
        col = jnp.sum(jnp.where(lane == h, cs, 0.0), axis=1, keepdims=True) * LOG2E
        hi, mid, lo = [piece.astype(F32) for piece in _split3(col)]
        a = lane - HEAD_DIM * (1 - h % HEADS_PER_BLOCK)
        one = jnp.where((a >= 3) & (a < 6), 1.0, 0.0)
        aux_q = jnp.where(a == 0, hi, jnp.where(a == 1, mid, jnp.where(a == 2, lo, one)))
        one = jnp.where((a >= 0) & (a < 3), 1.0, 0.0)
        aux_k = jnp.where(a == 3, -hi, jnp.where(a == 4, -mid, jnp.where(a == 5, -lo, one)))
        own = lane // HEAD_DIM == h % HEADS_PER_BLOCK
        p = h // HEADS_PER_BLOCK
        qa_ref[0, :, h * LANES:(h + 1) * LANES] = jnp.where(own, q[:, p * LANES:(p + 1) * LANES], aux_q.astype(BF16))
        ka_ref[0, :, h * LANES:(h + 1) * LANES] = jnp.where(own, k[:, p * LANES:(p + 1) * LANES], aux_k.astype(BF16))


def _fox_pack(flog, fqk, tc=512):
    b, s, _ = flog.shape
    n_pairs = WA // LANES
    aug = jax.ShapeDtypeStruct((b, s, FOX_HEADS * LANES), BF16)
    return pl.pallas_call(
        functools.partial(_fox_pack_kernel, tc=tc),
        out_shape=(aug, aug),
        grid=(b, s // tc),
        in_specs=[pl.BlockSpec((1, tc, LANES), lambda bi, i: (bi, i, 0)),
                  pl.BlockSpec((1, tc, WA), lambda bi, i: (bi, i, 0)),
                  pl.BlockSpec((1, tc, WA), lambda bi, i: (bi, i, 1))],
        out_specs=(pl.BlockSpec((1, tc, FOX_HEADS * LANES), lambda bi, i: (bi, i, 0)),
                   pl.BlockSpec((1, tc, FOX_HEADS * LANES), lambda bi, i: (bi, i, 0))),
        scratch_shapes=[pltpu.VMEM((1, LANES), F32)],
        compiler_params=_cparams(("parallel", "arbitrary")),
        name="fox_pack",
    )(flog, fqk, fqk)


def _attend(qs, k_refs, vt_ref, acc_refs, m_refs, i, t):
    key = lax.broadcasted_iota(jnp.int32, (t, t), 0)
    qry = lax.broadcasted_iota(jnp.int32, (t, t), 1)
    causal = key <= qry
    heads = range(HEADS_PER_BLOCK)
    ones_rows = jnp.where(lax.broadcasted_iota(jnp.int32, (SUM_ROWS, t), 0) == 0, 1.0, 0.0).astype(BF16)

    def keys(h, j):
        return k_refs[h][0, pl.ds(pl.multiple_of(j * t, t), t), :]

    def values(vt, h):
        return jnp.concatenate([vt[h * HEAD_DIM:(h + 1) * HEAD_DIM], ones_rows], axis=0)

    vt = vt_ref[0, 0, i]
    for h in heads:
        s = jnp.where(causal, _dot_nt(keys(h, i), qs[h]), -jnp.inf)
        m = jnp.max(s, axis=0, keepdims=True)
        m_refs[h][...] = m
        acc_refs[h][...] = _dot(values(vt, h), jnp.exp2(s - m).astype(BF16))

    def sweep(nc, first, last):
        def body(j):
            ms = [m_refs[h][...] for h in heads]
            ss = [_dot_nt(k_refs[h][0, pl.ds(pl.multiple_of(j * (nc * t), nc * t), nc * t), :], qs[h])
                  for h in heads]
            cmax = [jnp.max(ss[h], axis=0, keepdims=True) for h in heads]
            ok = jnp.maximum(jnp.max(cmax[0] - ms[0]), jnp.max(cmax[1] - ms[1])) <= EXP2_HEADROOM
            new = []
            for h in heads:
                p = jnp.exp2(ss[h] - ms[h]).astype(BF16)
                pv = [_dot(values(vt_ref[0, 0, j * nc + c], h), p[c * t:(c + 1) * t]) for c in range(nc)]
                new.append(acc_refs[h][...] + functools.reduce(jnp.add, pv))

            @pl.when(ok)
            def _():
                for h in heads:
                    acc_refs[h][...] = new[h]

            @pl.when(jnp.logical_not(ok))
            def _():
                for h in heads:
                    m_new = jnp.maximum(ms[h], cmax[h])
                    acc_refs[h][...] = jnp.exp2(ms[h] - m_new) * acc_refs[h][...]
                    m_refs[h][...] = m_new

            return j + ok.astype(jnp.int32)

        lax.while_loop(lambda j: j < last, body, jnp.asarray(first, jnp.int32))

    n_groups = i // CHUNKS_PER_STEP
    sweep(CHUNKS_PER_STEP, 0, n_groups)
    sweep(1, n_groups * CHUNKS_PER_STEP, i)
    o_t = jnp.concatenate([acc_refs[h][:HEAD_DIM] / acc_refs[h][HEAD_DIM:HEAD_DIM + 1] for h in heads], axis=0)
    return o_t.T


def _attn_scratch(t):
    return ([pltpu.VMEM((HEAD_DIM + SUM_ROWS, t), F32)] * HEADS_PER_BLOCK
            + [pltpu.VMEM((1, t), F32)] * HEADS_PER_BLOCK)


def _head_masks(shape):
    lane = lax.broadcasted_iota(jnp.int32, shape, 1)
    return [lane // HEAD_DIM == h for h in range(HEADS_PER_BLOCK)]


def _fox_kernel(q0_ref, q1_ref, k0_ref, k1_ref, vt_ref, o_ref, a0, a1, m0, m1, *, t):
    o = _attend((q0_ref[0], q1_ref[0]), (k0_ref, k1_ref), vt_ref, (a0, a1), (m0, m1), pl.program_id(2), t)
    o_ref[0] = o.astype(o_ref.dtype)


def _fox_attention(qa, ka, vt, t=T_ATTN):
    b, s, _ = qa.shape
    n_pairs = FOX_HEADS // HEADS_PER_BLOCK
    qspec = lambda h: pl.BlockSpec((1, t, LANES), lambda bi, p, i: (bi, i, HEADS_PER_BLOCK * p + h))
    kspec = lambda h: pl.BlockSpec((1, s, LANES), lambda bi, p, i: (bi, 0, HEADS_PER_BLOCK * p + h))
    return pl.pallas_call(
        functools.partial(_fox_kernel, t=t),
        out_shape=jax.ShapeDtypeStruct((b, s, WA), BF16),
        grid=(b, n_pairs, s // t),
        in_specs=[qspec(0), qspec(1), kspec(0), kspec(1),
                  pl.BlockSpec((1, 1, s // t, LANES, t), lambda bi, p, i: (bi, p, 0, 0, 0))],
        out_specs=pl.BlockSpec((1, t, LANES), lambda bi, p, i: (bi, i, p)),
        scratch_shapes=_attn_scratch(t),
        compiler_params=_cparams(("parallel", "parallel", "arbitrary")),
        name="fox_attn",
    )(qa, qa, ka, ka, vt)


def _dil_kernel(q_ref, kc_ref, kp_ref, vc_ref, vp_ref, o_ref, lse_ref, *, t, span, prev):
    i = pl.program_id(3)
    q = q_ref[0]
    kc, kp, vc, vp = kc_ref[0], kp_ref[0], vc_ref[0], vp_ref[0]
    hm = _head_masks((t, LANES))
    r_c = lax.broadcasted_iota(jnp.int32, (t, t), 0)
    c_c = lax.broadcasted_iota(jnp.int32, (t, t), 1)
    dist_c = r_c - c_c
    mask_c = (dist_c >= 0) & (dist_c <= span)
    r_p = lax.broadcasted_iota(jnp.int32, (t, prev), 0)
    c_p = lax.broadcasted_iota(jnp.int32, (t, prev), 1)
    dist_p = r_p - c_p + prev + jnp.where(i > 0, 0, 2 * span + 2)
    mask_p = dist_p <= span
    outs, lses = [], []
    for h in range(HEADS_PER_BLOCK):
        qh = jnp.where(hm[h], q, jnp.zeros_like(q))
        s_c = jnp.where(mask_c, _dot_nt(qh, kc), -jnp.inf)
        s_p = jnp.where(mask_p, _dot_nt(qh, kp), -jnp.inf)
        m = jnp.maximum(jnp.max(s_c, axis=1, keepdims=True), jnp.max(s_p, axis=1, keepdims=True))
        p_c = jnp.exp(s_c - m)
        p_p = jnp.exp(s_p - m)
        den = jnp.sum(p_c, axis=1, keepdims=True) + jnp.sum(p_p, axis=1, keepdims=True)
        o = (_dot(p_c.astype(BF16), vc) + _dot(p_p.astype(BF16), vp)) / den
        outs.append(o)
        lses.append(m + jnp.log(den))
    o_ref[0] = jnp.where(hm[0], outs[0], outs[1])
    lse_ref[0] = jnp.where(hm[0], lses[0], lses[1])


def _dilated_group(dq, dkv, group, window, dil, t):
    b, s, _ = dq.shape
    span = window // dil
    prev = LANES
    assert span <= prev and s % (dil * t) == 0 and t % prev == 0
    n_kv_pairs = DIL_KV_HEADS // HEADS_PER_BLOCK
    qb_per_row = WBQ // LANES
    kvb_per_row = 2 * WBK // LANES
    ob_per_row = WBK // LANES
    sd = s // dil
    dq_v = dq.reshape(b, sd, dil * WBQ)
    dkv_v = dkv.reshape(b, sd, dil * 2 * WBK)
    tp = t // prev
    cur = lambda col: (lambda bi, r, p, i: (bi, i, col(r, p)))
    prv = lambda col: (lambda bi, r, p, i: (bi, jnp.maximum(i * tp - 1, 0), col(r, p)))
    qcol = lambda r, p: r * qb_per_row + group * n_kv_pairs + p
    kcol = lambda r, p: r * kvb_per_row + p
    vcol = lambda r, p: r * kvb_per_row + n_kv_pairs + p
    ocol = lambda r, p: r * ob_per_row + p
    o, lse = pl.pallas_call(
        functools.partial(_dil_kernel, t=t, span=span, prev=prev),
        out_shape=(jax.ShapeDtypeStruct((b, sd, dil * WBK), F32),
                   jax.ShapeDtypeStruct((b, sd, dil * WBK), F32)),
        grid=(b, dil, n_kv_pairs, sd // t),
        in_specs=[pl.BlockSpec((1, t, LANES), cur(qcol)),
                  pl.BlockSpec((1, t, LANES), cur(kcol)),
                  pl.BlockSpec((1, prev, LANES), prv(kcol)),
                  pl.BlockSpec((1, t, LANES), cur(vcol)),
                  pl.BlockSpec((1, prev, LANES), prv(vcol))],
        out_specs=(pl.BlockSpec((1, t, LANES), cur(ocol)),
                   pl.BlockSpec((1, t, LANES), cur(ocol))),
        compiler_params=_cparams(("parallel", "parallel", "parallel", "parallel")),
        name=f"dilated_attn_g{group}",
    )(dq_v, dkv_v, dkv_v, dkv_v, dkv_v)
    return o.reshape(b * s, WBK), lse.reshape(b * s, WBK)


def _moba_select_bias(gate, n_past, lane_off):
    lane = lax.broadcasted_iota(jnp.int32, gate.shape, 1)
    blk = lane - lane_off
    g = jnp.where((blk >= 0) & (blk < n_past), gate, -jnp.inf)
    sel = blk == n_past
    for _ in range(MOBA_TOPK):
        mx = jnp.max(g, axis=1, keepdims=True)
        is_max = (g == mx) & (mx > -jnp.inf)
        idx = jnp.min(jnp.where(is_max, lane, 2 * LANES), axis=1, keepdims=True)
        pick = lane == idx
        sel = sel | pick
        g = jnp.where(pick, -jnp.inf, g)
    return jnp.where(sel, 0.0, NEG_BIG)


def _moba_kernel(q_ref, k0_ref, k1_ref, vt_ref, km_ref, o_ref, a0, a1, m0, m1, *, t):
    i = pl.program_id(2)
    q = q_ref[0]
    hm = _head_masks((t, LANES))
    km = km_ref[0]
    nb = km.shape[0]
    own_blk = (i * t + lax.broadcasted_iota(jnp.int32, (t, 1), 0)) // MOBA_BLOCK
    qaug = []
    for h in range(HEADS_PER_BLOCK):
        qh = jnp.where(hm[h], q, jnp.zeros_like(q))
        lane_off = HEAD_DIM * (1 - h)
        pads = (lane_off, LANES - lane_off - nb)
        kmat = jnp.concatenate([jnp.zeros((pads[0], LANES), F32)] * (pads[0] > 0) + [km]
                               + [jnp.zeros((pads[1], LANES), F32)] * (pads[1] > 0), axis=0)
        k_hi, k_mid, k_lo = _split3(kmat)
        gate = _dot_nt(qh, k_hi) + _dot_nt(qh, k_mid) + _dot_nt(qh, k_lo)
        bias = _moba_select_bias(gate, own_blk, lane_off)
        qaug.append(jnp.where(hm[h], q, bias.astype(BF16)))
    o = _attend(qaug, (k0_ref, k1_ref), vt_ref, (a0, a1), (m0, m1), i, t)
    o_ref[0] = o.astype(o_ref.dtype)


def _moba_attention(mq, mk, mvt, km, t=T_ATTN):
    b, s, _ = mq.shape
    assert t % MOBA_BLOCK == 0
    n_pairs = MOBA_HEADS // HEADS_PER_BLOCK
    nb = s // MOBA_BLOCK
    kspec = lambda h: pl.BlockSpec((1, s, LANES), lambda bi, p, i: (bi, 0, HEADS_PER_BLOCK * p + h))
    return pl.pallas_call(
        functools.partial(_moba_kernel, t=t),
        out_shape=jax.ShapeDtypeStruct((b, s, WC), BF16),
        grid=(b, n_pairs, s // t),
        in_specs=[pl.BlockSpec((1, t, LANES), lambda bi, p, i: (bi, i, p)),
                  kspec(0), kspec(1),
                  pl.BlockSpec((1, 1, s // t, LANES, t), lambda bi, p, i: (bi, p, 0, 0, 0)),
                  pl.BlockSpec((1, nb, LANES), lambda bi, p, i: (bi, 0, p))],
        out_specs=pl.BlockSpec((1, t, LANES), lambda bi, p, i: (bi, i, p)),
        scratch_shapes=_attn_scratch(t),
        compiler_params=_cparams(("parallel", "parallel", "arbitrary")),
        name="moba_attn",
    )(mq, mk, mk, mvt, km)


def _sigmoid(z):
    return 1.0 / (1.0 + jnp.exp(-z))


def _merge_kernel(x_ref, ya_ref, yc_ref, o0_ref, o1_ref, o2_ref, e0_ref, e1_ref, e2_ref,
                  wg_ref, wpa_ref, wpb_ref, wpc_ref, wo_ref, g_ref, b_ref, out_ref):
    x = x_ref[...]
    xb = x.astype(BF16)
    e0, e1, e2 = e0_ref[...], e1_ref[...], e2_ref[...]
    mx = jnp.maximum(jnp.maximum(e0, e1), e2)
    w0, w1, w2 = jnp.exp(e0 - mx), jnp.exp(e1 - mx), jnp.exp(e2 - mx)
    yb = (w0 * o0_ref[...] + w1 * o1_ref[...] + w2 * o2_ref[...]) / (w0 + w1 + w2)
    merged = (_sigmoid(_dot(xb, wg_ref[:, :D_MODEL])) * _dot(ya_ref[...], wpa_ref[...])
              + _sigmoid(_dot(xb, wg_ref[:, D_MODEL:2 * D_MODEL])) * _dot(yb.astype(BF16), wpb_ref[...])
              + _sigmoid(_dot(xb, wg_ref[:, 2 * D_MODEL:])) * _dot(yc_ref[...], wpc_ref[...]))
    mix = _dot(merged.astype(BF16), wo_ref[...])
    out_ref[...] = _layer_norm_rows(DN_ALPHA * x + mix, g_ref[...], b_ref[...])


def _merge(x2d, ya, yc, dil_o, dil_lse, wg, wpa, wpb, wpc, wo, g, b, tm=256):
    m, d = x2d.shape
    row = lambda i: (i, 0)
    full = lambda i: (0, 0)
    rows = lambda w: pl.BlockSpec((tm, w), row)
    whole = lambda a: pl.BlockSpec(a.shape, full)
    return pl.pallas_call(
        _merge_kernel,
        out_shape=jax.ShapeDtypeStruct((m, d), F32),
        grid=(m // tm,),
        in_specs=[rows(d), rows(WA), rows(WC)] + [rows(WBK)] * 6
                 + [whole(wg), whole(wpa), whole(wpb), whole(wpc), whole(wo),
                    pl.BlockSpec((1, d), full), pl.BlockSpec((1, d), full)],
        out_specs=rows(d),
        compiler_params=_cparams(("parallel",)),
        name="merge_proj_ln",
    )(x2d, ya, yc, *dil_o, *dil_lse, wg, wpa, wpb, wpc, wo, g.reshape(1, d), b.reshape(1, d))


def _ffn_kernel(x_ref, wg_ref, wu_ref, wd_ref, g_ref, b_ref, out_ref, acc_ref):
    f = pl.program_id(1)
    x = x_ref[...]
    xb = x.astype(BF16)
    hg = _dot(xb, wg_ref[...])
    hid = hg * _sigmoid(hg) * _dot(xb, wu_ref[...])
    part = _dot(hid.astype(BF16), wd_ref[...])

    @pl.when(f == 0)
    def _():
        acc_ref[...] = part

    @pl.when(f > 0)
    def _():
        acc_ref[...] += part

    @pl.when(f == pl.num_programs(1) - 1)
    def _():
        out_ref[...] = _layer_norm_rows(DN_ALPHA * x + acc_ref[...], g_ref[...], b_ref[...])


def _ffn(x2d, wg, wu, wd, g, b, tm=512, n_f=2):
    m, d = x2d.shape
    tf = D_FF // n_f
    assert tf % LANES == 0
    return pl.pallas_call(
        _ffn_kernel,
        out_shape=jax.ShapeDtypeStruct((m, d), F32),
        grid=(m // tm, n_f),
        in_specs=[pl.BlockSpec((tm, d), lambda i, f: (i, 0)),
                  pl.BlockSpec((d, tf), lambda i, f: (0, f)),
                  pl.BlockSpec((d, tf), lambda i, f: (0, f)),
                  pl.BlockSpec((tf, d), lambda i, f: (f, 0)),
                  pl.BlockSpec((1, d), lambda i, f: (0, 0)),
                  pl.BlockSpec((1, d), lambda i, f: (0, 0))],
        out_specs=pl.BlockSpec((tm, d), lambda i, f: (i, 0)),
        scratch_shapes=[pltpu.VMEM((tm, d), F32)],
        compiler_params=_cparams(("parallel", "arbitrary")),
        name="swiglu_ln",
    )(x2d, wg, wu, wd, g.reshape(1, d), b.reshape(1, d))


def _rope_tables(seq):
    half = ROPE_DIM // 2
    inv_freq = jnp.power(ROPE_THETA, -jnp.arange(half, dtype=F32) * 2.0 / ROPE_DIM)
    ang = jnp.arange(seq).astype(F32)[:, None] * inv_freq[None, :]
    cos, sin = jnp.cos(ang), jnp.sin(ang)
    ones = jnp.ones((seq, HEAD_DIM - ROPE_DIM), F32)
    zeros = jnp.zeros((seq, HEAD_DIM - ROPE_DIM), F32)
    zh = jnp.zeros((seq, half), F32)
    c = jnp.concatenate([cos, cos, ones], axis=1)
    s1 = jnp.concatenate([-sin, zh, zeros], axis=1)
    s2 = jnp.concatenate([zh, sin, zeros], axis=1)
    tile = lambda a: jnp.concatenate([a] * HEADS_PER_BLOCK, axis=1)
    return tile(c), tile(s1), tile(s2)


def _pack_w_in(w, b_f):
    sizes = (WA, WA, WA, FOX_HEADS, WBQ, WBK, WBK, WC, WC, WC, N_GATE)
    offs = np.cumsum(sizes)[:-1].tolist()
    qa, ka, va, fa, qb, kb, vb, qc, kc, vc, g = jnp.split(w, offs, axis=1)
    fa_pad = jnp.pad(fa, ((0, 0), (0, LANES - FOX_HEADS)))
    w_cat = jnp.concatenate([qa * SCALE, ka, va, fa_pad, qb * SCALE, kb, vb, qc * SCALE, kc, vc], axis=1)
    bf_pad = jnp.pad(b_f, (0, LANES - FOX_HEADS)).reshape(1, LANES)
    return w_cat.astype(BF16), g.astype(BF16), bf_pad


def kernel(x, ln_in_g, ln_in_b, w_in, b_f, w_pa, w_pb, w_pc, w_o, ln1_g, ln1_b,
           w_gate, w_up, w_down, ln2_g, ln2_b):
    bsz, seq, d = x.shape
    m = bsz * seq
    rope_tabs = _rope_tables(seq)
    h = _layer_norm(x.reshape(m, d), ln_in_g, ln_in_b)
    for l in range(DEPTH):
        w_cat, w_g, bf_pad = _pack_w_in(w_in[l], b_f[l])
        fqk, fvt, flog, dq, dkv, mq, mk, mvt, km = _in_proj(h, w_cat, bf_pad, rope_tabs, bsz, seq)
        shp = lambda a: a.reshape(bsz, seq, a.shape[-1])
        qa, ka = _fox_pack(shp(flog), shp(fqk))
        ya = _fox_attention(qa, ka, fvt)
        dil = [_dilated_group(shp(dq), shp(dkv), g, window, dil_, T_DIL)
               for g, (window, dil_) in enumerate(DIL_PATTERNS)]
        yc = _moba_attention(shp(mq), shp(mk), mvt, km.reshape(bsz, seq // MOBA_BLOCK, WC))
        h = _merge(h, ya.reshape(m, WA), yc.reshape(m, WC), [o for o, _ in dil], [e for _, e in dil],
                   w_g, w_pa[l].astype(BF16), w_pb[l].astype(BF16), w_pc[l].astype(BF16),
                   w_o[l].astype(BF16), ln1_g[l], ln1_b[l])
        h = _ffn(h, w_gate[l].astype(BF16), w_up[l].astype(BF16), w_down[l].astype(BF16),
                 ln2_g[l], ln2_b[l])
    return h.reshape(bsz, seq, d)
```

```python
import functools

import jax
import jax.numpy as jnp
import numpy as np
from jax import lax
from jax.experimental import pallas as pl
from jax.experimental.pallas import tpu as pltpu

D_MODEL = 1024
DEPTH = 2
HEAD_DIM = 64
FOX_HEADS = 6
DIL_KV_HEADS = 4
DIL_PATTERNS = ((128, 1), (512, 4), (2048, 16))
DIL_Q_HEADS = DIL_KV_HEADS * len(DIL_PATTERNS)
MOBA_HEADS = 6
MOBA_BLOCK = 256
MOBA_TOPK = 3
ROPE_THETA = 500000.0
ROPE_DIM = HEAD_DIM // 4
N_BRANCH = 3
D_FF = -(-(8 * D_MODEL) // (3 * 256)) * 256
LN_EPS = 1e-5
DN_ALPHA = (2 * DEPTH) ** 0.25
SCALE = HEAD_DIM ** -0.5

LANES = 128
HEADS_PER_BLOCK = LANES // HEAD_DIM
VMEM_LIMIT = 48 * 1024 * 1024

WA = FOX_HEADS * HEAD_DIM
WBQ = DIL_Q_HEADS * HEAD_DIM
WBK = DIL_KV_HEADS * HEAD_DIM
WC = MOBA_HEADS * HEAD_DIM
N_GATE = N_BRANCH * D_MODEL
SEG_FOX = (0, 3 * WA)
SEG_F = (SEG_FOX[1], SEG_FOX[1] + LANES)
SEG_DQ = (SEG_F[1], SEG_F[1] + WBQ)
SEG_DK = (SEG_DQ[1], SEG_DQ[1] + WBK)
SEG_DV = (SEG_DK[1], SEG_DK[1] + WBK)
SEG_MQ = (SEG_DV[1], SEG_DV[1] + WC)
SEG_MK = (SEG_MQ[1], SEG_MQ[1] + WC)
SEG_MV = (SEG_MK[1], SEG_MK[1] + WC)
N_PROJ = SEG_MV[1]

T_ATTN = 2 * MOBA_BLOCK
T_DIL = 2048
DIL_SUB = 256
NEG_BIG = -1e30
LOG2E = 1.4426950408889634
EXP2_HEADROOM = 64.0
CHUNKS_PER_STEP = 4
SUM_ROWS = 16
BF16 = jnp.bfloat16
F32 = jnp.float32


def _cparams(sem):
    return pltpu.CompilerParams(dimension_semantics=sem, vmem_limit_bytes=VMEM_LIMIT)


def _dot(a, b):
    return jnp.dot(a, b, preferred_element_type=F32)


def _dot_nt(a, b):
    return lax.dot_general(a, b, (((1,), (1,)), ((), ())), preferred_element_type=F32)


def _split3(x):
    hi = x.astype(BF16)
    r1 = x - hi.astype(F32)
    mid = r1.astype(BF16)
    lo = (r1 - mid.astype(F32)).astype(BF16)
    return hi, mid, lo


def _layer_norm_rows(y, g, b):
    mu = jnp.mean(y, axis=-1, keepdims=True)
    yc = y - mu
    var = jnp.mean(yc * yc, axis=-1, keepdims=True)
    return yc * lax.rsqrt(var + LN_EPS) * g + b


def _ln_kernel(x_ref, g_ref, b_ref, o_ref):
    o_ref[...] = _layer_norm_rows(x_ref[...], g_ref[...], b_ref[...])


def _layer_norm(x2d, g, b, tm=512):
    m, d = x2d.shape
    return pl.pallas_call(
        _ln_kernel,
        out_shape=jax.ShapeDtypeStruct((m, d), F32),
        grid=(m // tm,),
        in_specs=[pl.BlockSpec((tm, d), lambda i: (i, 0)),
                  pl.BlockSpec((1, d), lambda i: (0, 0)),
                  pl.BlockSpec((1, d), lambda i: (0, 0))],
        out_specs=pl.BlockSpec((tm, d), lambda i: (i, 0)),
        compiler_params=_cparams(("parallel",)),
        name="ln_in",
    )(x2d, g.reshape(1, d), b.reshape(1, d))


def _rope_block(t, c, s1, s2):
    return t * c + pltpu.roll(t, LANES - ROPE_DIM // 2, 1) * s1 + pltpu.roll(t, ROPE_DIM // 2, 1) * s2


def _in_proj_kernel(x_ref, w_ref, bf_ref, rc_ref, rs1_ref, rs2_ref,
                    fq_ref, fk_ref, fvt_ref, dq_ref, dkv_ref, mq_ref, mk_ref, mvt_ref, km_ref, carry_ref,
                    *, tm, tk, n_moba_blocks, tiles_per_seq):
    xb = x_ref[...].astype(BF16)
    rc, rs1, rs2 = rc_ref[...], rs1_ref[...], rs2_ref[...]

    def proj(seg):
        return _dot(xb, w_ref[:, seg[0]:seg[1]])

    def rope(t):
        return jnp.concatenate(
            [_rope_block(t[:, c:c + LANES], rc, rs1, rs2) for c in range(0, t.shape[1], LANES)], axis=1)

    def store_transposed(vt_ref, v):
        for p in range(v.shape[1] // LANES):
            vt = v[:, p * LANES:(p + 1) * LANES].T
            for c in range(tm // tk):
                vt_ref[0, p, c] = vt[:, c * tk:(c + 1) * tk].astype(BF16)

    fox = proj(SEG_FOX)
    store_transposed(fvt_ref, fox[:, 2 * WA:])
    z = proj(SEG_F) + bf_ref[...]
    flog = jnp.minimum(z, 0.0) - jnp.log1p(jnp.exp(-jnp.abs(z)))
    _fox_pack(flog, fox[:, :WA], fox[:, WA:2 * WA], fq_ref, fk_ref, carry_ref,
              pl.program_id(0) % tiles_per_seq == 0)

    dq_ref[...] = rope(proj(SEG_DQ))
    dkv_ref[:, :WBK] = rope(proj(SEG_DK))
    dkv_ref[:, WBK:] = proj(SEG_DV)

    mq_ref[...] = (rope(proj(SEG_MQ)) * LOG2E).astype(BF16)
    store_transposed(mvt_ref, proj(SEG_MV))
    kc = rope(proj(SEG_MK))
    for r in range(tm // MOBA_BLOCK):
        km_ref[r] = jnp.sum(kc[r * MOBA_BLOCK:(r + 1) * MOBA_BLOCK], axis=0, keepdims=True) * (1.0 / MOBA_BLOCK)
    row = pl.program_id(0) * tm + lax.broadcasted_iota(jnp.int32, (tm, LANES), 0)
    blk = (row // MOBA_BLOCK) % n_moba_blocks
    lane = lax.broadcasted_iota(jnp.int32, (tm, LANES), 1)
    lo = lane < HEAD_DIM
    oh_hi = jnp.where(lane - HEAD_DIM == blk, 1.0, 0.0)
    oh_lo = jnp.where(lane == blk, 1.0, 0.0)
    for p in range(WC // LANES):
        kp = kc[:, p * LANES:(p + 1) * LANES]
        mk_ref[:, (2 * p) * LANES:(2 * p + 1) * LANES] = jnp.where(lo, kp, oh_hi).astype(BF16)
        mk_ref[:, (2 * p + 1) * LANES:(2 * p + 2) * LANES] = jnp.where(lo, oh_lo, kp).astype(BF16)


def _in_proj(x2d, w_cat, bf_pad, rope_tabs, bsz, seq, tm=512, tk=T_ATTN):
    m, d = x2d.shape
    n_rt = seq // tm
    n_moba_blocks = seq // MOBA_BLOCK
    assert n_moba_blocks <= HEAD_DIM and seq % tm == 0 and tm % MOBA_BLOCK == 0 and tm % tk == 0
    row = lambda i: (i, 0)
    tab = lambda i: (i % n_rt, 0)
    vt_shape = lambda w: jax.ShapeDtypeStruct((bsz, w // LANES, seq // tk, LANES, tk), BF16)
    vt_spec = lambda w: pl.BlockSpec((1, w // LANES, tm // tk, LANES, tk),
                                     lambda i: (i // n_rt, 0, i % n_rt, 0, 0))
    outs = (
        jax.ShapeDtypeStruct((m, FOX_HEADS * LANES), BF16),
        jax.ShapeDtypeStruct((m, FOX_HEADS * LANES), BF16),
        vt_shape(WA),
        jax.ShapeDtypeStruct((m, WBQ), F32),
        jax.ShapeDtypeStruct((m, 2 * WBK), F32),
        jax.ShapeDtypeStruct((m, WC), BF16),
        jax.ShapeDtypeStruct((m, 2 * WC), BF16),
        vt_shape(WC),
        jax.ShapeDtypeStruct((m // MOBA_BLOCK, 1, WC), F32),
    )
    return pl.pallas_call(
        functools.partial(_in_proj_kernel, tm=tm, tk=tk, n_moba_blocks=n_moba_blocks, tiles_per_seq=n_rt),
        out_shape=outs,
        grid=(m // tm,),
        in_specs=[pl.BlockSpec((tm, d), row),
                  pl.BlockSpec((d, N_PROJ), lambda i: (0, 0)),
                  pl.BlockSpec((1, LANES), lambda i: (0, 0)),
                  pl.BlockSpec((tm, LANES), tab),
                  pl.BlockSpec((tm, LANES), tab),
                  pl.BlockSpec((tm, LANES), tab)],
        out_specs=(pl.BlockSpec((tm, FOX_HEADS * LANES), row),
                   pl.BlockSpec((tm, FOX_HEADS * LANES), row),
                   vt_spec(WA),
                   pl.BlockSpec((tm, WBQ), row),
                   pl.BlockSpec((tm, 2 * WBK), row),
                   pl.BlockSpec((tm, WC), row),
                   pl.BlockSpec((tm, 2 * WC), row),
                   vt_spec(WC),
                   pl.BlockSpec((tm // MOBA_BLOCK, 1, WC), lambda i: (i, 0, 0))),
        scratch_shapes=[pltpu.VMEM((1, LANES), F32)],
        compiler_params=_cparams(("arbitrary",)),
        name="in_proj",
    )(x2d, w_cat, bf_pad, *rope_tabs)


def _fox_pack(flog, q, k, qa_ref, ka_ref, carry_ref, first_of_sequence):
    tc = flog.shape[0]

    @pl.when(first_of_sequence)
    def _():
        carry_ref[...] = jnp.zeros_like(carry_ref)

    r = lax.broadcasted_iota(jnp.int32, (tc, tc), 0)
    c = lax.broadcasted_iota(jnp.int32, (tc, tc), 1)
    tri = jnp.where(c <= r, 1.0, 0.0).astype(BF16)
    f_hi, f_mid, f_lo = _split3(flog)
    cs = _dot(tri, f_hi) + _dot(tri, f_mid) + _dot(tri, f_lo) + carry_ref[...]
    carry_ref[...] = cs[tc - 1:tc, :]

    lane = lax.broadcasted_iota(jnp.int32, (tc, LANES), 1)
    for h in range(FOX_HEADS):
        col = jnp.sum(jnp.where(lane == h, cs, 0.0), axis=1, keepdims=True) * LOG2E
        hi, mid, lo = [piece.astype(F32) for piece in _split3(col)]
        a = lane - HEAD_DIM * (1 - h % HEADS_PER_BLOCK)
        one = jnp.where((a >= 3) & (a < 6), 1.0, 0.0)
        aux_q = jnp.where(a == 0, hi, jnp.where(a == 1, mid, jnp.where(a == 2, lo, one)))
        one = jnp.where((a >= 0) & (a < 3), 1.0, 0.0)
        aux_k = jnp.where(a == 3, -hi, jnp.where(a == 4, -mid, jnp.where(a == 5, -lo, one)))
        own = lane // HEAD_DIM == h % HEADS_PER_BLOCK
        p = h // HEADS_PER_BLOCK
        qa_ref[:, h * LANES:(h + 1) * LANES] = jnp.where(own, q[:, p * LANES:(p + 1) * LANES] * LOG2E, aux_q).astype(BF16)
        ka_ref[:, h * LANES:(h + 1) * LANES] = jnp.where(own, k[:, p * LANES:(p + 1) * LANES], aux_k).astype(BF16)


def _attend(qs, k_refs, vt_ref, acc_refs, m_refs, i, t):
    key = lax.broadcasted_iota(jnp.int32, (t, t), 0)
    qry = lax.broadcasted_iota(jnp.int32, (t, t), 1)
    causal = key <= qry
    heads = range(HEADS_PER_BLOCK)
    ones_rows = jnp.where(lax.broadcasted_iota(jnp.int32, (SUM_ROWS, t), 0) == 0, 1.0, 0.0).astype(BF16)

    def keys(h, j):
        return k_refs[h][0, pl.ds(pl.multiple_of(j * t, t), t), :]

    def values(vt, h):
        return jnp.concatenate([vt[h * HEAD_DIM:(h + 1) * HEAD_DIM], ones_rows], axis=0)

    vt = vt_ref[0, 0, i]
    for h in heads:
        s = jnp.where(causal, _dot_nt(keys(h, i), qs[h]), -jnp.inf)
        m = jnp.max(s, axis=0, keepdims=True)
        m_refs[h][...] = m
        acc_refs[h][...] = _dot(values(vt, h), jnp.exp2(s - m).astype(BF16))

    def sweep(nc, first, last):
        def body(j):
            ms = [m_refs[h][...] for h in heads]
            ss = [_dot_nt(k_refs[h][0, pl.ds(pl.multiple_of(j * (nc * t), nc * t), nc * t), :], qs[h])
                  for h in heads]
            cmax = [jnp.max(ss[h], axis=0, keepdims=True) for h in heads]
            ok = jnp.maximum(jnp.max(cmax[0] - ms[0]), jnp.max(cmax[1] - ms[1])) <= EXP2_HEADROOM
            new = []
            for h in heads:
                p = jnp.exp2(ss[h] - ms[h]).astype(BF16)
                pv = [_dot(values(vt_ref[0, 0, j * nc + c], h), p[c * t:(c + 1) * t]) for c in range(nc)]
                new.append(acc_refs[h][...] + functools.reduce(jnp.add, pv))

            @pl.when(ok)
            def _():
                for h in heads:
                    acc_refs[h][...] = new[h]

            @pl.when(jnp.logical_not(ok))
            def _():
                for h in heads:
                    m_new = jnp.maximum(ms[h], cmax[h])
                    acc_refs[h][...] = jnp.exp2(ms[h] - m_new) * acc_refs[h][...]
                    m_refs[h][...] = m_new

            return j + ok.astype(jnp.int32)

        lax.while_loop(lambda j: j < last, body, jnp.asarray(first, jnp.int32))

    n_groups = i // CHUNKS_PER_STEP
    sweep(CHUNKS_PER_STEP, 0, n_groups)
    sweep(1, n_groups * CHUNKS_PER_STEP, i)
    o_t = jnp.concatenate([acc_refs[h][:HEAD_DIM] / acc_refs[h][HEAD_DIM:HEAD_DIM + 1] for h in heads], axis=0)
    return o_t.T


def _attn_scratch(t):
    return ([pltpu.VMEM((HEAD_DIM + SUM_ROWS, t), F32)] * HEADS_PER_BLOCK
            + [pltpu.VMEM((1, t), F32)] * HEADS_PER_BLOCK)


def _head_masks(shape):
    lane = lax.broadcasted_iota(jnp.int32, shape, 1)
    return [lane // HEAD_DIM == h for h in range(HEADS_PER_BLOCK)]


def _fox_kernel(q0_ref, q1_ref, k0_ref, k1_ref, vt_ref, o_ref, a0, a1, m0, m1, *, t):
    o = _attend((q0_ref[0], q1_ref[0]), (k0_ref, k1_ref), vt_ref, (a0, a1), (m0, m1), pl.program_id(2), t)
    o_ref[0] = o.astype(o_ref.dtype)


def _fox_attention(qa, ka, vt, t=T_ATTN):
    b, s, _ = qa.shape
    n_pairs = FOX_HEADS // HEADS_PER_BLOCK
    qspec = lambda h: pl.BlockSpec((1, t, LANES), lambda bi, p, i: (bi, i, HEADS_PER_BLOCK * p + h))
    kspec = lambda h: pl.BlockSpec((1, s, LANES), lambda bi, p, i: (bi, 0, HEADS_PER_BLOCK * p + h))
    return pl.pallas_call(
        functools.partial(_fox_kernel, t=t),
        out_shape=jax.ShapeDtypeStruct((b, s, WA), BF16),
        grid=(b, n_pairs, s // t),
        in_specs=[qspec(0), qspec(1), kspec(0), kspec(1),
                  pl.BlockSpec((1, 1, s // t, LANES, t), lambda bi, p, i: (bi, p, 0, 0, 0))],
        out_specs=pl.BlockSpec((1, t, LANES), lambda bi, p, i: (bi, i, p)),
        scratch_shapes=_attn_scratch(t),
        compiler_params=_cparams(("parallel", "parallel", "arbitrary")),
        name="fox_attn",
    )(qa, qa, ka, ka, vt)


def _dil_band(q, k, v, span, at_seq_start):
    nq, nk = q.shape[0], k.shape[0]
    kk = lax.broadcasted_iota(jnp.int32, (nk, nq), 0)
    qq = lax.broadcasted_iota(jnp.int32, (nk, nq), 1)
    dist = qq + span - kk
    valid = (dist >= 0) & (dist <= jnp.where(at_seq_start, jnp.minimum(qq, span), span))
    hm = _head_masks((nq, LANES))
    qb, kb = q.astype(BF16), k.astype(BF16)
    vt = v.T.astype(BF16)
    rows = []
    for h in range(HEADS_PER_BLOCK):
        s = jnp.where(valid, _dot_nt(kb, jnp.where(hm[h], qb, jnp.zeros_like(qb))), -jnp.inf)
        m = jnp.max(s, axis=0, keepdims=True)
        p = jnp.exp(s - m)
        den = jnp.sum(p, axis=0, keepdims=True)
        rows.append((_dot(vt[h * HEAD_DIM:(h + 1) * HEAD_DIM], p.astype(BF16)) / den,
                     jnp.broadcast_to(m + jnp.log(den), (HEAD_DIM, nq))))
    both = jnp.concatenate([r[0] for r in rows] + [r[1] for r in rows], axis=0).T
    return both[:, :LANES], both[:, LANES:]


def _dil_kernel(q0_ref, q1_ref, q2_ref, kc_ref, kp_ref, vc_ref, vp_ref, y_ref, o_scr, e_scr, *, tq, sub):
    at_start = pl.program_id(2) == 0
    q_refs = (q0_ref, q1_ref, q2_ref)
    for g, (window, dil) in enumerate(DIL_PATTERNS):
        span = window // dil
        rows = tq // dil
        nq = min(sub, rows)

        def take(ref, r, first, n):
            idx = pl.ds(first * dil + r, n, stride=dil) if dil > 1 else pl.ds(first, n)
            return ref[0, idx, :]

        for r in range(dil):
            for u in range(rows // nq):
                if u == 0:
                    k = jnp.concatenate([take(kp_ref, r, rows - span, span), take(kc_ref, r, 0, nq)], axis=0)
                    v = jnp.concatenate([take(vp_ref, r, rows - span, span), take(vc_ref, r, 0, nq)], axis=0)
                else:
                    k = take(kc_ref, r, u * nq - span, nq + span)
                    v = take(vc_ref, r, u * nq - span, nq + span)
                o, e = _dil_band(take(q_refs[g], r, u * nq, nq), k, v, span, at_start if u == 0 else False)
                idx = pl.ds(u * nq * dil + r, nq, stride=dil) if dil > 1 else pl.ds(u * nq, nq)
                o_scr[g, idx, :] = o
                e_scr[g, idx, :] = e
    e = [e_scr[g] for g in range(len(DIL_PATTERNS))]
    mx = functools.reduce(jnp.maximum, e)
    w = [jnp.exp(eg - mx) for eg in e]
    num = functools.reduce(jnp.add, [w[g] * o_scr[g] for g in range(len(DIL_PATTERNS))])
    y_ref[0] = (num / functools.reduce(jnp.add, w)).astype(y_ref.dtype)


def _dilated_attention(dq, dkv, tq=T_DIL, sub=DIL_SUB):
    b, s, _ = dq.shape
    n_groups = len(DIL_PATTERNS)
    n_kv_pairs = DIL_KV_HEADS // HEADS_PER_BLOCK
    for window, dil in DIL_PATTERNS:
        span, rows = window // dil, tq // dil
        assert window % dil == 0 and tq % dil == 0 and span <= rows and rows % min(sub, rows) == 0
        assert span % 16 == 0 and min(sub, rows) % LANES == 0
    assert s % tq == 0
    cur = lambda col: (lambda bi, p, i: (bi, i, col + p))
    prv = lambda col: (lambda bi, p, i: (bi, jnp.maximum(i - 1, 0), col + p))
    blk = lambda imap: pl.BlockSpec((1, tq, LANES), imap)
    return pl.pallas_call(
        functools.partial(_dil_kernel, tq=tq, sub=sub),
        out_shape=jax.ShapeDtypeStruct((b, s, WBK), BF16),
        grid=(b, n_kv_pairs, s // tq),
        in_specs=[blk(cur(g * n_kv_pairs)) for g in range(n_groups)]
                 + [blk(cur(0)), blk(prv(0)), blk(cur(n_kv_pairs)), blk(prv(n_kv_pairs))],
        out_specs=blk(cur(0)),
        scratch_shapes=[pltpu.VMEM((n_groups, tq, LANES), F32)] * 2,
        compiler_params=_cparams(("parallel", "parallel", "arbitrary")),
        name="dilated_attn",
    )(*([dq] * n_groups), dkv, dkv, dkv, dkv)


def _moba_select_bias(gate, n_past, row_off):
    row = lax.broadcasted_iota(jnp.int32, gate.shape, 0)
    blk = row - row_off
    g = jnp.where((blk >= 0) & (blk < n_past), gate, -jnp.inf)
    sel = blk == n_past
    for _ in range(MOBA_TOPK):
        mx = jnp.max(g, axis=0, keepdims=True)
        is_max = (g == mx) & (mx > -jnp.inf)
        idx = jnp.min(jnp.where(is_max, row, 2 * LANES), axis=0, keepdims=True)
        pick = row == idx
        sel = sel | pick
        g = jnp.where(pick, -jnp.inf, g)
    return jnp.where(sel, 0.0, NEG_BIG)


def _moba_kernel(q_ref, k0_ref, k1_ref, vt_ref, km_ref, o_ref, a0, a1, m0, m1, *, t):
    i = pl.program_id(2)
    q = q_ref[0]
    hm = _head_masks((t, LANES))
    km = km_ref[0]
    nb = km.shape[0]
    own_blk = (i * t + lax.broadcasted_iota(jnp.int32, (1, t), 1)) // MOBA_BLOCK
    qaug = []
    for h in range(HEADS_PER_BLOCK):
        qh = jnp.where(hm[h], q, jnp.zeros_like(q))
        row_off = HEAD_DIM * (1 - h)
        pads = (row_off, LANES - row_off - nb)
        kmat = jnp.concatenate([jnp.zeros((pads[0], LANES), F32)] * (pads[0] > 0) + [km]
                               + [jnp.zeros((pads[1], LANES), F32)] * (pads[1] > 0), axis=0)
        k_hi, k_mid, k_lo = _split3(kmat)
        gate = _dot_nt(k_hi, qh) + _dot_nt(k_mid, qh) + _dot_nt(k_lo, qh)
        bias = _moba_select_bias(gate, own_blk, row_off).T
        qaug.append(jnp.where(hm[h], q, bias.astype(BF16)))
    o = _attend(qaug, (k0_ref, k1_ref), vt_ref, (a0, a1), (m0, m1), i, t)
    o_ref[0] = o.astype(o_ref.dtype)


def _moba_attention(mq, mk, mvt, km, t=T_ATTN):
    b, s, _ = mq.shape
    assert t % MOBA_BLOCK == 0
    n_pairs = MOBA_HEADS // HEADS_PER_BLOCK
    nb = s // MOBA_BLOCK
    kspec = lambda h: pl.BlockSpec((1, s, LANES), lambda bi, p, i: (bi, 0, HEADS_PER_BLOCK * p + h))
    return pl.pallas_call(
        functools.partial(_moba_kernel, t=t),
        out_shape=jax.ShapeDtypeStruct((b, s, WC), BF16),
        grid=(b, n_pairs, s // t),
        in_specs=[pl.BlockSpec((1, t, LANES), lambda bi, p, i: (bi, i, p)),
                  kspec(0), kspec(1),
                  pl.BlockSpec((1, 1, s // t, LANES, t), lambda bi, p, i: (bi, p, 0, 0, 0)),
                  pl.BlockSpec((1, nb, LANES), lambda bi, p, i: (bi, 0, p))],
        out_specs=pl.BlockSpec((1, t, LANES), lambda bi, p, i: (bi, i, p)),
        scratch_shapes=_attn_scratch(t),
        compiler_params=_cparams(("parallel", "parallel", "arbitrary")),
        name="moba_attn",
    )(mq, mk, mk, mvt, km)


def _sigmoid(z):
    return 1.0 / (1.0 + jnp.exp(-z))


def _merge_kernel(x_ref, ya_ref, yb_ref, yc_ref,
                  wg_ref, wpa_ref, wpb_ref, wpc_ref, wo_ref, g_ref, b_ref, out_ref):
    x = x_ref[...]
    xb = x.astype(BF16)
    merged = (_sigmoid(_dot(xb, wg_ref[:, :D_MODEL])) * _dot(ya_ref[...], wpa_ref[...])
              + _sigmoid(_dot(xb, wg_ref[:, D_MODEL:2 * D_MODEL])) * _dot(yb_ref[...], wpb_ref[...])
              + _sigmoid(_dot(xb, wg_ref[:, 2 * D_MODEL:])) * _dot(yc_ref[...], wpc_ref[...]))
    mix = _dot(merged.astype(BF16), wo_ref[...])
    out_ref[...] = _layer_norm_rows(DN_ALPHA * x + mix, g_ref[...], b_ref[...])


def _merge(x2d, ya, yb, yc, wg, wpa, wpb, wpc, wo, g, b, tm=256):
    m, d = x2d.shape
    row = lambda i: (i, 0)
    full = lambda i: (0, 0)
    rows = lambda w: pl.BlockSpec((tm, w), row)
    whole = lambda a: pl.BlockSpec(a.shape, full)
    return pl.pallas_call(
        _merge_kernel,
        out_shape=jax.ShapeDtypeStruct((m, d), F32),
        grid=(m // tm,),
        in_specs=[rows(d), rows(WA), rows(WBK), rows(WC),
                  whole(wg), whole(wpa), whole(wpb), whole(wpc), whole(wo),
                  pl.BlockSpec((1, d), full), pl.BlockSpec((1, d), full)],
        out_specs=rows(d),
        compiler_params=_cparams(("parallel",)),
        name="merge_proj_ln",
    )(x2d, ya, yb, yc, wg, wpa, wpb, wpc, wo, g.reshape(1, d), b.reshape(1, d))


def _ffn_kernel(x_ref, wg_ref, wu_ref, wd_ref, g_ref, b_ref, out_ref, acc_ref):
    f = pl.program_id(1)
    x = x_ref[...]
    xb = x.astype(BF16)
    hg = _dot(xb, wg_ref[...])
    hid = hg * _sigmoid(hg) * _dot(xb, wu_ref[...])
    part = _dot(hid.astype(BF16), wd_ref[...])

    @pl.when(f == 0)
    def _():
        acc_ref[...] = part

    @pl.when(f > 0)
    def _():
        acc_ref[...] += part

    @pl.when(f == pl.num_programs(1) - 1)
    def _():
        out_ref[...] = _layer_norm_rows(DN_ALPHA * x + acc_ref[...], g_ref[...], b_ref[...])


def _ffn(x2d, wg, wu, wd, g, b, tm=512, n_f=2):
    m, d = x2d.shape
    tf = D_FF // n_f
    assert tf % LANES == 0
    return pl.pallas_call(
        _ffn_kernel,
        out_shape=jax.ShapeDtypeStruct((m, d), F32),
        grid=(m // tm, n_f),
        in_specs=[pl.BlockSpec((tm, d), lambda i, f: (i, 0)),
                  pl.BlockSpec((d, tf), lambda i, f: (0, f)),
                  pl.BlockSpec((d, tf), lambda i, f: (0, f)),
                  pl.BlockSpec((tf, d), lambda i, f: (f, 0)),
                  pl.BlockSpec((1, d), lambda i, f: (0, 0)),
                  pl.BlockSpec((1, d), lambda i, f: (0, 0))],
        out_specs=pl.BlockSpec((tm, d), lambda i, f: (i, 0)),
        scratch_shapes=[pltpu.VMEM((tm, d), F32)],
        compiler_params=_cparams(("parallel", "arbitrary")),
        name="swiglu_ln",
    )(x2d, wg, wu, wd, g.reshape(1, d), b.reshape(1, d))


def _rope_tables(seq):
    half = ROPE_DIM // 2
    inv_freq = jnp.power(ROPE_THETA, -jnp.arange(half, dtype=F32) * 2.0 / ROPE_DIM)
    dim = np.arange(LANES) % HEAD_DIM
    ang = jnp.arange(seq).astype(F32)[:, None] * inv_freq[dim % half][None, :]
    cos, sin = jnp.cos(ang), jnp.sin(ang)
    first, second = (dim < half)[None, :], ((dim >= half) & (dim < ROPE_DIM))[None, :]
    return (jnp.where(first | second, cos, 1.0), jnp.where(first, -sin, 0.0), jnp.where(second, sin, 0.0))


def _pack_w_in(w, b_f):
    sizes = (WA, WA, WA, FOX_HEADS, WBQ, WBK, WBK, WC, WC, WC, N_GATE)
    offs = np.cumsum(sizes)[:-1].tolist()
    qa, ka, va, fa, qb, kb, vb, qc, kc, vc, g = jnp.split(w, offs, axis=1)
    fa_pad = jnp.pad(fa, ((0, 0), (0, LANES - FOX_HEADS)))
    w_cat = jnp.concatenate([qa * SCALE, ka, va, fa_pad, qb * SCALE, kb, vb, qc * SCALE, kc, vc], axis=1)
    bf_pad = jnp.pad(b_f, (0, LANES - FOX_HEADS)).reshape(1, LANES)
    return w_cat.astype(BF16), g.astype(BF16), bf_pad


def kernel(x, ln_in_g, ln_in_b, w_in, b_f, w_pa, w_pb, w_pc, w_o, ln1_g, ln1_b,
           w_gate, w_up, w_down, ln2_g, ln2_b):
    bsz, seq, d = x.shape
    m = bsz * seq
    rope_tabs = _rope_tables(seq)
    h = _layer_norm(x.reshape(m, d), ln_in_g, ln_in_b)
    for l in range(DEPTH):
        w_cat, w_g, bf_pad = _pack_w_in(w_in[l], b_f[l])
        qa, ka, fvt, dq, dkv, mq, mk, mvt, km = _in_proj(h, w_cat, bf_pad, rope_tabs, bsz, seq)
        shp = lambda a: a.reshape(bsz, seq, a.shape[-1])
        ya = _fox_attention(shp(qa), shp(ka), fvt)
        yb = _dilated_attention(shp(dq), shp(dkv))
        yc = _moba_attention(shp(mq), shp(mk), mvt, km.reshape(bsz, seq // MOBA_BLOCK, WC))
        h = _merge(h, ya.reshape(m, WA), yb.reshape(m, WBK), yc.reshape(m, WC),
                   w_g, w_pa[l].astype(BF16), w_pb[l].astype(BF16), w_pc[l].astype(BF16),
                   w_o[l].astype(BF16), ln1_g[l], ln1_b[l])
        h = _ffn(h, w_gate[l].astype(BF16), w_up[l].astype(BF16), w_down[l].astype(BF16),
                 ln2_g[l], ln2_b[l])
    return h.reshape(bsz, seq, d)
```

```python
import functools

import jax
import jax.numpy as jnp
import numpy as np
from jax import lax
from jax.experimental import pallas as pl
from jax.experimental.pallas import tpu as pltpu

D_MODEL = 1024
DEPTH = 2
HEAD_DIM = 64
FOX_HEADS = 6
DIL_KV_HEADS = 4
DIL_PATTERNS = ((128, 1), (512, 4), (2048, 16))
DIL_Q_HEADS = DIL_KV_HEADS * len(DIL_PATTERNS)
MOBA_HEADS = 6
MOBA_BLOCK = 256
MOBA_TOPK = 3
ROPE_THETA = 500000.0
ROPE_DIM = HEAD_DIM // 4
N_BRANCH = 3
D_FF = -(-(8 * D_MODEL) // (3 * 256)) * 256
LN_EPS = 1e-5
DN_ALPHA = (2 * DEPTH) ** 0.25
SCALE = HEAD_DIM ** -0.5

LANES = 128
HEADS_PER_BLOCK = LANES // HEAD_DIM
VMEM_LIMIT = 48 * 1024 * 1024

WA = FOX_HEADS * HEAD_DIM
WBQ = DIL_Q_HEADS * HEAD_DIM
WBK = DIL_KV_HEADS * HEAD_DIM
WC = MOBA_HEADS * HEAD_DIM
N_GATE = N_BRANCH * D_MODEL
SEG_FOX = (0, 3 * WA)
SEG_F = (SEG_FOX[1], SEG_FOX[1] + LANES)
SEG_DQ = (SEG_F[1], SEG_F[1] + WBQ)
SEG_DK = (SEG_DQ[1], SEG_DQ[1] + WBK)
SEG_DV = (SEG_DK[1], SEG_DK[1] + WBK)
SEG_MQ = (SEG_DV[1], SEG_DV[1] + WC)
SEG_MK = (SEG_MQ[1], SEG_MQ[1] + WC)
SEG_MV = (SEG_MK[1], SEG_MK[1] + WC)
N_PROJ = SEG_MV[1]

T_ATTN = 2 * MOBA_BLOCK
T_DIL = 2048
DIL_SUB = 256
DIL_BATCH = 8
NEG_BIG = -1e30
LOG2E = 1.4426950408889634
EXP2_HEADROOM = 64.0
CHUNK_GROUPS = (8, 4, 2, 1)
SUM_ROWS = 16
BF16 = jnp.bfloat16
F32 = jnp.float32


def _cparams(sem):
    return pltpu.CompilerParams(dimension_semantics=sem, vmem_limit_bytes=VMEM_LIMIT)


def _dot(a, b):
    return jnp.dot(a, b, preferred_element_type=F32)


def _dot_nt(a, b):
    return lax.dot_general(a, b, (((1,), (1,)), ((), ())), preferred_element_type=F32)


def _split3(x):
    hi = x.astype(BF16)
    r1 = x - hi.astype(F32)
    mid = r1.astype(BF16)
    lo = (r1 - mid.astype(F32)).astype(BF16)
    return hi, mid, lo


def _layer_norm_rows(y, g, b):
    mu = jnp.mean(y, axis=-1, keepdims=True)
    yc = y - mu
    var = jnp.mean(yc * yc, axis=-1, keepdims=True)
    return yc * lax.rsqrt(var + LN_EPS) * g + b


def _ln_kernel(x_ref, g_ref, b_ref, o_ref):
    o_ref[...] = _layer_norm_rows(x_ref[...], g_ref[...], b_ref[...])


def _layer_norm(x2d, g, b, tm=512):
    m, d = x2d.shape
    return pl.pallas_call(
        _ln_kernel,
        out_shape=jax.ShapeDtypeStruct((m, d), F32),
        grid=(m // tm,),
        in_specs=[pl.BlockSpec((tm, d), lambda i: (i, 0)),
                  pl.BlockSpec((1, d), lambda i: (0, 0)),
                  pl.BlockSpec((1, d), lambda i: (0, 0))],
        out_specs=pl.BlockSpec((tm, d), lambda i: (i, 0)),
        compiler_params=_cparams(("parallel",)),
        name="ln_in",
    )(x2d, g.reshape(1, d), b.reshape(1, d))


def _rope_block(t, c, s1, s2):
    return t * c + pltpu.roll(t, LANES - ROPE_DIM // 2, 1) * s1 + pltpu.roll(t, ROPE_DIM // 2, 1) * s2


def _in_proj_kernel(x_ref, w_ref, bf_ref, rc_ref, rs1_ref, rs2_ref,
                    fq_ref, fk_ref, fvt_ref, dq_ref, dkv_ref, mq_ref, mk_ref, mvt_ref, km_ref, carry_ref,
                    *, tm, tk, n_moba_blocks, tiles_per_seq):
    xb = x_ref[...].astype(BF16)
    rc, rs1, rs2 = rc_ref[...], rs1_ref[...], rs2_ref[...]

    def proj(seg):
        return _dot(xb, w_ref[:, seg[0]:seg[1]])

    def rope(t):
        return jnp.concatenate(
            [_rope_block(t[:, c:c + LANES], rc, rs1, rs2) for c in range(0, t.shape[1], LANES)], axis=1)

    def store_transposed(vt_ref, v):
        for p in range(v.shape[1] // LANES):
            vt = v[:, p * LANES:(p + 1) * LANES].T
            for c in range(tm // tk):
                vt_ref[0, p, c] = vt[:, c * tk:(c + 1) * tk].astype(BF16)

    fox = proj(SEG_FOX)
    store_transposed(fvt_ref, fox[:, 2 * WA:])
    z = proj(SEG_F) + bf_ref[...]
    flog = jnp.minimum(z, 0.0) - jnp.log1p(jnp.exp(-jnp.abs(z)))
    _fox_pack(flog, fox[:, :WA], fox[:, WA:2 * WA], fq_ref, fk_ref, carry_ref,
              pl.program_id(0) % tiles_per_seq == 0)

    dq_ref[...] = rope(proj(SEG_DQ)) * LOG2E
    dkv_ref[:, :WBK] = rope(proj(SEG_DK))
    dkv_ref[:, WBK:] = proj(SEG_DV)

    mq_ref[...] = (rope(proj(SEG_MQ)) * LOG2E).astype(BF16)
    store_transposed(mvt_ref, proj(SEG_MV))
    kc = rope(proj(SEG_MK))
    for r in range(tm // MOBA_BLOCK):
        km_ref[r] = jnp.sum(kc[r * MOBA_BLOCK:(r + 1) * MOBA_BLOCK], axis=0, keepdims=True) * (1.0 / MOBA_BLOCK)
    row = pl.program_id(0) * tm + lax.broadcasted_iota(jnp.int32, (tm, LANES), 0)
    blk = (row // MOBA_BLOCK) % n_moba_blocks
    lane = lax.broadcasted_iota(jnp.int32, (tm, LANES), 1)
    lo = lane < HEAD_DIM
    oh_hi = jnp.where(lane - HEAD_DIM == blk, 1.0, 0.0)
    oh_lo = jnp.where(lane == blk, 1.0, 0.0)
    for p in range(WC // LANES):
        kp = kc[:, p * LANES:(p + 1) * LANES]
        mk_ref[:, (2 * p) * LANES:(2 * p + 1) * LANES] = jnp.where(lo, kp, oh_hi).astype(BF16)
        mk_ref[:, (2 * p + 1) * LANES:(2 * p + 2) * LANES] = jnp.where(lo, oh_lo, kp).astype(BF16)


def _in_proj(x2d, w_cat, bf_pad, rope_tabs, bsz, seq, tm=512, tk=T_ATTN):
    m, d = x2d.shape
    n_rt = seq // tm
    n_moba_blocks = seq // MOBA_BLOCK
    assert n_moba_blocks <= HEAD_DIM and seq % tm == 0 and tm % MOBA_BLOCK == 0 and tm % tk == 0
    row = lambda i: (i, 0)
    tab = lambda i: (i % n_rt, 0)
    vt_shape = lambda w: jax.ShapeDtypeStruct((bsz, w // LANES, seq // tk, LANES, tk), BF16)
    vt_spec = lambda w: pl.BlockSpec((1, w // LANES, tm // tk, LANES, tk),
                                     lambda i: (i // n_rt, 0, i % n_rt, 0, 0))
    outs = (
        jax.ShapeDtypeStruct((m, FOX_HEADS * LANES), BF16),
        jax.ShapeDtypeStruct((m, FOX_HEADS * LANES), BF16),
        vt_shape(WA),
        jax.ShapeDtypeStruct((m, WBQ), F32),
        jax.ShapeDtypeStruct((m, 2 * WBK), F32),
        jax.ShapeDtypeStruct((m, WC), BF16),
        jax.ShapeDtypeStruct((m, 2 * WC), BF16),
        vt_shape(WC),
        jax.ShapeDtypeStruct((m // MOBA_BLOCK, 1, WC), F32),
    )
    return pl.pallas_call(
        functools.partial(_in_proj_kernel, tm=tm, tk=tk, n_moba_blocks=n_moba_blocks, tiles_per_seq=n_rt),
        out_shape=outs,
        grid=(m // tm,),
        in_specs=[pl.BlockSpec((tm, d), row),
                  pl.BlockSpec((d, N_PROJ), lambda i: (0, 0)),
                  pl.BlockSpec((1, LANES), lambda i: (0, 0)),
                  pl.BlockSpec((tm, LANES), tab),
                  pl.BlockSpec((tm, LANES), tab),
                  pl.BlockSpec((tm, LANES), tab)],
        out_specs=(pl.BlockSpec((tm, FOX_HEADS * LANES), row),
                   pl.BlockSpec((tm, FOX_HEADS * LANES), row),
                   vt_spec(WA),
                   pl.BlockSpec((tm, WBQ), row),
                   pl.BlockSpec((tm, 2 * WBK), row),
                   pl.BlockSpec((tm, WC), row),
                   pl.BlockSpec((tm, 2 * WC), row),
                   vt_spec(WC),
                   pl.BlockSpec((tm // MOBA_BLOCK, 1, WC), lambda i: (i, 0, 0))),
        scratch_shapes=[pltpu.VMEM((1, LANES), F32)],
        compiler_params=_cparams(("arbitrary",)),
        name="in_proj",
    )(x2d, w_cat, bf_pad, *rope_tabs)


def _fox_pack(flog, q, k, qa_ref, ka_ref, carry_ref, first_of_sequence):
    tc = flog.shape[0]

    @pl.when(first_of_sequence)
    def _():
        carry_ref[...] = jnp.zeros_like(carry_ref)

    r = lax.broadcasted_iota(jnp.int32, (tc, tc), 0)
    c = lax.broadcasted_iota(jnp.int32, (tc, tc), 1)
    tri = jnp.where(c <= r, 1.0, 0.0).astype(BF16)
    f_hi, f_mid, f_lo = _split3(flog)
    cs = _dot(tri, f_hi) + _dot(tri, f_mid) + _dot(tri, f_lo) + carry_ref[...]
    carry_ref[...] = cs[tc - 1:tc, :]

    lane = lax.broadcasted_iota(jnp.int32, (tc, LANES), 1)
    for h in range(FOX_HEADS):
        col = jnp.sum(jnp.where(lane == h, cs, 0.0), axis=1, keepdims=True) * LOG2E
        hi, mid, lo = [piece.astype(F32) for piece in _split3(col)]
        a = lane - HEAD_DIM * (1 - h % HEADS_PER_BLOCK)
        one = jnp.where((a >= 3) & (a < 6), 1.0, 0.0)
        aux_q = jnp.where(a == 0, hi, jnp.where(a == 1, mid, jnp.where(a == 2, lo, one)))
        one = jnp.where((a >= 0) & (a < 3), 1.0, 0.0)
        aux_k = jnp.where(a == 3, -hi, jnp.where(a == 4, -mid, jnp.where(a == 5, -lo, one)))
        own = lane // HEAD_DIM == h % HEADS_PER_BLOCK
        p = h // HEADS_PER_BLOCK
        qa_ref[:, h * LANES:(h + 1) * LANES] = jnp.where(own, q[:, p * LANES:(p + 1) * LANES] * LOG2E, aux_q).astype(BF16)
        ka_ref[:, h * LANES:(h + 1) * LANES] = jnp.where(own, k[:, p * LANES:(p + 1) * LANES], aux_k).astype(BF16)


def _attend(qs, k_refs, vt_ref, acc_refs, m_refs, i, t):
    key = lax.broadcasted_iota(jnp.int32, (t, t), 0)
    qry = lax.broadcasted_iota(jnp.int32, (t, t), 1)
    causal = key <= qry
    heads = range(HEADS_PER_BLOCK)
    def keys(h, j):
        return k_refs[h][0, pl.ds(pl.multiple_of(j * t, t), t), :]

    def values(vt, h):
        ones_rows = jnp.where(lax.broadcasted_iota(jnp.int32, (SUM_ROWS, vt.shape[1]), 0) == 0, 1.0, 0.0)
        return jnp.concatenate([vt[h * HEAD_DIM:(h + 1) * HEAD_DIM], ones_rows.astype(BF16)], axis=0)

    vt = vt_ref[0, 0, i]
    ss = [jnp.where(causal, _dot_nt(keys(h, i), qs[h]), -jnp.inf) for h in heads]
    ms = [jnp.max(s, axis=0, keepdims=True) for s in ss]
    ps = [jnp.exp2(s - m).astype(BF16) for s, m in zip(ss, ms)]
    for h in heads:
        m_refs[h][...] = ms[h]
        acc_refs[h][...] = _dot(values(vt, h), ps[h])

    def sweep(nc, first, last):
        def body(j):
            ms = [m_refs[h][...] for h in heads]
            ss = [_dot_nt(k_refs[h][0, pl.ds(pl.multiple_of(j * (nc * t), nc * t), nc * t), :], qs[h])
                  for h in heads]
            cmax = [jnp.max(ss[h], axis=0, keepdims=True) for h in heads]
            ok = jnp.maximum(jnp.max(cmax[0] - ms[0]), jnp.max(cmax[1] - ms[1])) <= EXP2_HEADROOM
            vt = jnp.concatenate([vt_ref[0, 0, j * nc + c] for c in range(nc)], axis=1)
            new = [acc_refs[h][...] + _dot(values(vt, h), jnp.exp2(ss[h] - ms[h]).astype(BF16)) for h in heads]

            @pl.when(ok)
            def _():
                for h in heads:
                    acc_refs[h][...] = new[h]

            @pl.when(jnp.logical_not(ok))
            def _():
                for h in heads:
                    m_new = jnp.maximum(ms[h], cmax[h])
                    acc_refs[h][...] = jnp.exp2(ms[h] - m_new) * acc_refs[h][...]
                    m_refs[h][...] = m_new

            return j + ok.astype(jnp.int32)

        lax.while_loop(lambda j: j < last, body, jnp.asarray(first, jnp.int32))

    done = 0
    for nc in CHUNK_GROUPS:
        sweep(nc, done // nc, i // nc)
        done = (i // nc) * nc
    o_t = jnp.concatenate([acc_refs[h][:HEAD_DIM] / acc_refs[h][HEAD_DIM:HEAD_DIM + 1] for h in heads], axis=0)
    return o_t.T


def _attn_scratch(t):
    return ([pltpu.VMEM((HEAD_DIM + SUM_ROWS, t), F32)] * HEADS_PER_BLOCK
            + [pltpu.VMEM((1, t), F32)] * HEADS_PER_BLOCK)


def _head_masks(shape):
    lane = lax.broadcasted_iota(jnp.int32, shape, 1)
    return [lane // HEAD_DIM == h for h in range(HEADS_PER_BLOCK)]


def _fox_kernel(q0_ref, q1_ref, k0_ref, k1_ref, vt_ref, o_ref, a0, a1, m0, m1, *, t):
    o = _attend((q0_ref[0], q1_ref[0]), (k0_ref, k1_ref), vt_ref, (a0, a1), (m0, m1), pl.program_id(2), t)
    o_ref[0] = o.astype(o_ref.dtype)


def _fox_attention(qa, ka, vt, t=T_ATTN):
    b, s, _ = qa.shape
    n_pairs = FOX_HEADS // HEADS_PER_BLOCK
    qspec = lambda h: pl.BlockSpec((1, t, LANES), lambda bi, p, i: (bi, i, HEADS_PER_BLOCK * p + h))
    kspec = lambda h: pl.BlockSpec((1, s, LANES), lambda bi, p, i: (bi, 0, HEADS_PER_BLOCK * p + h),
                                   pipeline_mode=pl.Buffered(1))
    return pl.pallas_call(
        functools.partial(_fox_kernel, t=t),
        out_shape=jax.ShapeDtypeStruct((b, s, WA), BF16),
        grid=(b, n_pairs, s // t),
        in_specs=[qspec(0), qspec(1), kspec(0), kspec(1),
                  pl.BlockSpec((1, 1, s // t, LANES, t), lambda bi, p, i: (bi, p, 0, 0, 0),
                               pipeline_mode=pl.Buffered(1))],
        out_specs=pl.BlockSpec((1, t, LANES), lambda bi, p, i: (bi, i, p)),
        scratch_shapes=_attn_scratch(t),
        compiler_params=_cparams(("parallel", "parallel", "arbitrary")),
        name="fox_attn",
    )(qa, qa, ka, ka, vt)


def _dil_band_bias(nk, nq, span, at_seq_start):
    kk = lax.broadcasted_iota(jnp.int32, (nk, nq), 0)
    qq = lax.broadcasted_iota(jnp.int32, (nk, nq), 1)
    dist = qq + span - kk
    valid = (dist >= 0) & (dist <= jnp.where(at_seq_start, jnp.minimum(qq, span), span))
    return jnp.where(valid, 0.0, -jnp.inf)


def _dil_bands(problems):
    heads = range(HEADS_PER_BLOCK)
    logits, vts = [], []
    for q, k, v, bias in problems:
        hm = _head_masks(q.shape)
        qb, kb = q.astype(BF16), k.astype(BF16)
        logits.append([_dot_nt(kb, jnp.where(hm[h], qb, jnp.zeros_like(qb))) + bias for h in heads])
        vts.append(v.T.astype(BF16))
    probs, lses, dens = [], [], []
    for ss in logits:
        ms = [jnp.max(s, axis=0, keepdims=True) for s in ss]
        ps = [jnp.exp2(s - m) for s, m in zip(ss, ms)]
        den = [jnp.sum(p, axis=0, keepdims=True) for p in ps]
        probs.append([p.astype(BF16) for p in ps])
        dens.append(den)
        lses.append([m + jnp.log2(d) for m, d in zip(ms, den)])
    outs = []
    for vt, ps, den, lse in zip(vts, probs, dens, lses):
        nq = ps[0].shape[1]
        o_t = [_dot(vt[h * HEAD_DIM:(h + 1) * HEAD_DIM], ps[h]) / den[h] for h in heads]
        e_t = [jnp.broadcast_to(lse[h], (HEAD_DIM, nq)) for h in heads]
        both = jnp.concatenate(o_t + e_t, axis=0).T
        outs.append((both[:, :LANES], both[:, LANES:]))
    return outs


def _dil_kernel(q0_ref, q1_ref, q2_ref, kc_ref, kp_ref, vc_ref, vp_ref, y_ref, o_scr, e_scr, *, tq, sub):
    at_start = pl.program_id(2) == 0
    q_refs = (q0_ref, q1_ref, q2_ref)
    problems, dests = [], []
    for g, (window, dil) in enumerate(DIL_PATTERNS):
        span = window // dil
        rows = tq // dil
        nq = min(sub, rows)
        bias = _dil_band_bias(nq + span, nq, span, False)
        bias_first = _dil_band_bias(nq + span, nq, span, at_start)

        def take(ref, r, first, n):
            idx = pl.ds(first * dil + r, n, stride=dil) if dil > 1 else pl.ds(first, n)
            return ref[0, idx, :]

        for r in range(dil):
            for u in range(rows // nq):
                if u == 0:
                    k = jnp.concatenate([take(kp_ref, r, rows - span, span), take(kc_ref, r, 0, nq)], axis=0)
                    v = jnp.concatenate([take(vp_ref, r, rows - span, span), take(vc_ref, r, 0, nq)], axis=0)
                else:
                    k = take(kc_ref, r, u * nq - span, nq + span)
                    v = take(vc_ref, r, u * nq - span, nq + span)
                problems.append((take(q_refs[g], r, u * nq, nq), k, v, bias_first if u == 0 else bias))
                dests.append((g, pl.ds(u * nq * dil + r, nq, stride=dil) if dil > 1 else pl.ds(u * nq, nq)))
                if len(problems) == DIL_BATCH:
                    for (gd, idx), (o, e) in zip(dests, _dil_bands(problems)):
                        o_scr[gd, idx, :] = o
                        e_scr[gd, idx, :] = e
                    problems, dests = [], []
    assert not problems
    e = [e_scr[g] for g in range(len(DIL_PATTERNS))]
    mx = functools.reduce(jnp.maximum, e)
    w = [jnp.exp2(eg - mx) for eg in e]
    num = functools.reduce(jnp.add, [w[g] * o_scr[g] for g in range(len(DIL_PATTERNS))])
    y_ref[0] = (num / functools.reduce(jnp.add, w)).astype(y_ref.dtype)


def _dilated_attention(dq, dkv, tq=T_DIL, sub=DIL_SUB):
    b, s, _ = dq.shape
    n_groups = len(DIL_PATTERNS)
    n_kv_pairs = DIL_KV_HEADS // HEADS_PER_BLOCK
    for window, dil in DIL_PATTERNS:
        span, rows = window // dil, tq // dil
        assert window % dil == 0 and tq % dil == 0 and span <= rows and rows % min(sub, rows) == 0
        assert span % 16 == 0 and min(sub, rows) % LANES == 0
    assert s % tq == 0
    cur = lambda col: (lambda bi, p, i: (bi, i, col + p))
    prv = lambda col: (lambda bi, p, i: (bi, jnp.maximum(i - 1, 0), col + p))
    blk = lambda imap: pl.BlockSpec((1, tq, LANES), imap)
    return pl.pallas_call(
        functools.partial(_dil_kernel, tq=tq, sub=sub),
        out_shape=jax.ShapeDtypeStruct((b, s, WBK), BF16),
        grid=(b, n_kv_pairs, s // tq),
        in_specs=[blk(cur(g * n_kv_pairs)) for g in range(n_groups)]
                 + [blk(cur(0)), blk(prv(0)), blk(cur(n_kv_pairs)), blk(prv(n_kv_pairs))],
        out_specs=blk(cur(0)),
        scratch_shapes=[pltpu.VMEM((n_groups, tq, LANES), F32)] * 2,
        compiler_params=_cparams(("parallel", "parallel", "arbitrary")),
        name="dilated_attn",
    )(*([dq] * n_groups), dkv, dkv, dkv, dkv)


def _moba_select_bias(gate, n_past, row_off):
    row = lax.broadcasted_iota(jnp.int32, gate.shape, 0)
    blk = row - row_off
    g = jnp.where((blk >= 0) & (blk < n_past), gate, -jnp.inf)
    sel = blk == n_past
    for _ in range(MOBA_TOPK):
        mx = jnp.max(g, axis=0, keepdims=True)
        is_max = (g == mx) & (mx > -jnp.inf)
        idx = jnp.min(jnp.where(is_max, row, 2 * LANES), axis=0, keepdims=True)
        pick = row == idx
        sel = sel | pick
        g = jnp.where(pick, -jnp.inf, g)
    return jnp.where(sel, 0.0, NEG_BIG)


def _moba_kernel(q_ref, k0_ref, k1_ref, vt_ref, km_ref, o_ref, a0, a1, m0, m1, *, t):
    i = pl.program_id(2)
    q = q_ref[0]
    hm = _head_masks((t, LANES))
    km = km_ref[0]
    nb = km.shape[0]
    own_blk = (i * t + lax.broadcasted_iota(jnp.int32, (1, t), 1)) // MOBA_BLOCK
    qaug = []
    for h in range(HEADS_PER_BLOCK):
        qh = jnp.where(hm[h], q, jnp.zeros_like(q))
        row_off = HEAD_DIM * (1 - h)
        pads = (row_off, LANES - row_off - nb)
        kmat = jnp.concatenate([jnp.zeros((pads[0], LANES), F32)] * (pads[0] > 0) + [km]
                               + [jnp.zeros((pads[1], LANES), F32)] * (pads[1] > 0), axis=0)
        k_hi, k_mid, k_lo = _split3(kmat)
        gate = _dot_nt(k_hi, qh) + _dot_nt(k_mid, qh) + _dot_nt(k_lo, qh)
        bias = _moba_select_bias(gate, own_blk, row_off).T
        qaug.append(jnp.where(hm[h], q, bias.astype(BF16)))
    o = _attend(qaug, (k0_ref, k1_ref), vt_ref, (a0, a1), (m0, m1), i, t)
    o_ref[0] = o.astype(o_ref.dtype)


def _moba_attention(mq, mk, mvt, km, t=T_ATTN):
    b, s, _ = mq.shape
    assert t % MOBA_BLOCK == 0
    n_pairs = MOBA_HEADS // HEADS_PER_BLOCK
    nb = s // MOBA_BLOCK
    kspec = lambda h: pl.BlockSpec((1, s, LANES), lambda bi, p, i: (bi, 0, HEADS_PER_BLOCK * p + h),
                                   pipeline_mode=pl.Buffered(1))
    return pl.pallas_call(
        functools.partial(_moba_kernel, t=t),
        out_shape=jax.ShapeDtypeStruct((b, s, WC), BF16),
        grid=(b, n_pairs, s // t),
        in_specs=[pl.BlockSpec((1, t, LANES), lambda bi, p, i: (bi, i, p)),
                  kspec(0), kspec(1),
                  pl.BlockSpec((1, 1, s // t, LANES, t), lambda bi, p, i: (bi, p, 0, 0, 0),
                               pipeline_mode=pl.Buffered(1)),
                  pl.BlockSpec((1, nb, LANES), lambda bi, p, i: (bi, 0, p))],
        out_specs=pl.BlockSpec((1, t, LANES), lambda bi, p, i: (bi, i, p)),
        scratch_shapes=_attn_scratch(t),
        compiler_params=_cparams(("parallel", "parallel", "arbitrary")),
        name="moba_attn",
    )(mq, mk, mk, mvt, km)


def _sigmoid(z):
    return 1.0 / (1.0 + jnp.exp(-z))


def _merge_kernel(x_ref, ya_ref, yb_ref, yc_ref,
                  wg_ref, wpa_ref, wpb_ref, wpc_ref, wo_ref, g_ref, b_ref, out_ref):
    x = x_ref[...]
    xb = x.astype(BF16)
    merged = (_sigmoid(_dot(xb, wg_ref[:, :D_MODEL])) * _dot(ya_ref[...], wpa_ref[...])
              + _sigmoid(_dot(xb, wg_ref[:, D_MODEL:2 * D_MODEL])) * _dot(yb_ref[...], wpb_ref[...])
              + _sigmoid(_dot(xb, wg_ref[:, 2 * D_MODEL:])) * _dot(yc_ref[...], wpc_ref[...]))
    mix = _dot(merged.astype(BF16), wo_ref[...])
    out_ref[...] = _layer_norm_rows(DN_ALPHA * x + mix, g_ref[...], b_ref[...])


def _merge(x2d, ya, yb, yc, wg, wpa, wpb, wpc, wo, g, b, tm=512):
    m, d = x2d.shape
    row = lambda i: (i, 0)
    full = lambda i: (0, 0)
    rows = lambda w: pl.BlockSpec((tm, w), row)
    whole = lambda a: pl.BlockSpec(a.shape, full, pipeline_mode=pl.Buffered(1))
    return pl.pallas_call(
        _merge_kernel,
        out_shape=jax.ShapeDtypeStruct((m, d), F32),
        grid=(m // tm,),
        in_specs=[rows(d), rows(WA), rows(WBK), rows(WC),
                  whole(wg), whole(wpa), whole(wpb), whole(wpc), whole(wo),
                  pl.BlockSpec((1, d), full), pl.BlockSpec((1, d), full)],
        out_specs=rows(d),
        compiler_params=_cparams(("parallel",)),
        name="merge_proj_ln",
    )(x2d, ya, yb, yc, wg, wpa, wpb, wpc, wo, g.reshape(1, d), b.reshape(1, d))


def _ffn_kernel(x_ref, wg_ref, wu_ref, wd_ref, g_ref, b_ref, out_ref):
    x = x_ref[...]
    xb = x.astype(BF16)
    hg = _dot(xb, wg_ref[...])
    hid = hg * _sigmoid(hg) * _dot(xb, wu_ref[...])
    ffn = _dot(hid.astype(BF16), wd_ref[...])
    out_ref[...] = _layer_norm_rows(DN_ALPHA * x + ffn, g_ref[...], b_ref[...])


def _ffn(x2d, wg, wu, wd, g, b, tm=512):
    m, d = x2d.shape
    full = lambda i: (0, 0)
    whole = lambda a: pl.BlockSpec(a.shape, full, pipeline_mode=pl.Buffered(1))
    return pl.pallas_call(
        _ffn_kernel,
        out_shape=jax.ShapeDtypeStruct((m, d), F32),
        grid=(m // tm,),
        in_specs=[pl.BlockSpec((tm, d), lambda i: (i, 0)),
                  whole(wg), whole(wu), whole(wd),
                  pl.BlockSpec((1, d), full), pl.BlockSpec((1, d), full)],
        out_specs=pl.BlockSpec((tm, d), lambda i: (i, 0)),
        compiler_params=_cparams(("parallel",)),
        name="swiglu_ln",
    )(x2d, wg, wu, wd, g.reshape(1, d), b.reshape(1, d))


def _rope_tables(seq):
    half = ROPE_DIM // 2
    inv_freq = jnp.power(ROPE_THETA, -jnp.arange(half, dtype=F32) * 2.0 / ROPE_DIM)
    dim = np.arange(LANES) % HEAD_DIM
    ang = jnp.arange(seq).astype(F32)[:, None] * inv_freq[dim % half][None, :]
    cos, sin = jnp.cos(ang), jnp.sin(ang)
    first, second = (dim < half)[None, :], ((dim >= half) & (dim < ROPE_DIM))[None, :]
    return (jnp.where(first | second, cos, 1.0), jnp.where(first, -sin, 0.0), jnp.where(second, sin, 0.0))


def _pack_w_in(w, b_f):
    sizes = (WA, WA, WA, FOX_HEADS, WBQ, WBK, WBK, WC, WC, WC, N_GATE)
    offs = np.cumsum(sizes)[:-1].tolist()
    qa, ka, va, fa, qb, kb, vb, qc, kc, vc, g = jnp.split(w, offs, axis=1)
    fa_pad = jnp.pad(fa, ((0, 0), (0, LANES - FOX_HEADS)))
    w_cat = jnp.concatenate([qa * SCALE, ka, va, fa_pad, qb * SCALE, kb, vb, qc * SCALE, kc, vc], axis=1)
    bf_pad = jnp.pad(b_f, (0, LANES - FOX_HEADS)).reshape(1, LANES)
    return w_cat.astype(BF16), g.astype(BF16), bf_pad


def kernel(x, ln_in_g, ln_in_b, w_in, b_f, w_pa, w_pb, w_pc, w_o, ln1_g, ln1_b,
           w_gate, w_up, w_down, ln2_g, ln2_b):
    bsz, seq, d = x.shape
    m = bsz * seq
    rope_tabs = _rope_tables(seq)
    h = _layer_norm(x.reshape(m, d), ln_in_g, ln_in_b)
    for l in range(DEPTH):
        w_cat, w_g, bf_pad = _pack_w_in(w_in[l], b_f[l])
        qa, ka, fvt, dq, dkv, mq, mk, mvt, km = _in_proj(h, w_cat, bf_pad, rope_tabs, bsz, seq)
        shp = lambda a: a.reshape(bsz, seq, a.shape[-1])
        ya = _fox_attention(shp(qa), shp(ka), fvt)
        yb = _dilated_attention(shp(dq), shp(dkv))
        yc = _moba_attention(shp(mq), shp(mk), mvt, km.reshape(bsz, seq // MOBA_BLOCK, WC))
        h = _merge(h, ya.reshape(m, WA), yb.reshape(m, WBK), yc.reshape(m, WC),
                   w_g, w_pa[l].astype(BF16), w_pb[l].astype(BF16), w_pc[l].astype(BF16),
                   w_o[l].astype(BF16), ln1_g[l], ln1_b[l])
        h = _ffn(h, w_gate[l].astype(BF16), w_up[l].astype(BF16), w_down[l].astype(BF16),
                 ln2_g[l], ln2_b[l])
    return h.reshape(bsz, seq, d)
```

```python
import functools

import jax
import jax.numpy as jnp
import numpy as np
from jax import lax
from jax.experimental import pallas as pl
from jax.experimental.pallas import tpu as pltpu

D_MODEL = 1024
DEPTH = 2
HEAD_DIM = 64
FOX_HEADS = 6
DIL_KV_HEADS = 4
DIL_PATTERNS = ((128, 1), (512, 4), (2048, 16))
DIL_Q_HEADS = DIL_KV_HEADS * len(DIL_PATTERNS)
MOBA_HEADS = 6
MOBA_BLOCK = 256
MOBA_TOPK = 3
ROPE_THETA = 500000.0
ROPE_DIM = HEAD_DIM // 4
N_BRANCH = 3
D_FF = -(-(8 * D_MODEL) // (3 * 256)) * 256
LN_EPS = 1e-5
DN_ALPHA = (2 * DEPTH) ** 0.25
SCALE = HEAD_DIM ** -0.5

LANES = 128
HEADS_PER_BLOCK = LANES // HEAD_DIM
VMEM_LIMIT = 48 * 1024 * 1024

WA = FOX_HEADS * HEAD_DIM
WBQ = DIL_Q_HEADS * HEAD_DIM
WBK = DIL_KV_HEADS * HEAD_DIM
WC = MOBA_HEADS * HEAD_DIM
N_GATE = N_BRANCH * D_MODEL
SEG_FOX = (0, 3 * WA)
SEG_F = (SEG_FOX[1], SEG_FOX[1] + LANES)
SEG_DQ = (SEG_F[1], SEG_F[1] + WBQ)
SEG_DK = (SEG_DQ[1], SEG_DQ[1] + WBK)
SEG_DV = (SEG_DK[1], SEG_DK[1] + WBK)
SEG_MQ = (SEG_DV[1], SEG_DV[1] + WC)
SEG_MK = (SEG_MQ[1], SEG_MQ[1] + WC)
SEG_MV = (SEG_MK[1], SEG_MK[1] + WC)
N_PROJ = SEG_MV[1]

T_ATTN = 2 * MOBA_BLOCK
T_DIL = 2048
DIL_SUB = 256
DIL_BATCH = 8
NEG_BIG = -1e30
LOG2E = 1.4426950408889634
EXP2_HEADROOM = 64.0
SUM_ROWS = 16
CHUNK_GROUPS = (8, 4, 2, 1)
BF16 = jnp.bfloat16
F32 = jnp.float32


def _cparams(sem):
    return pltpu.CompilerParams(dimension_semantics=sem, vmem_limit_bytes=VMEM_LIMIT)


def _dot(a, b):
    return jnp.dot(a, b, preferred_element_type=F32)


def _dot_nt(a, b):
    return lax.dot_general(a, b, (((1,), (1,)), ((), ())), preferred_element_type=F32)


def _split3(x):
    hi = x.astype(BF16)
    r1 = x - hi.astype(F32)
    mid = r1.astype(BF16)
    lo = (r1 - mid.astype(F32)).astype(BF16)
    return hi, mid, lo


def _layer_norm_rows(y, g, b):
    mu = jnp.mean(y, axis=-1, keepdims=True)
    yc = y - mu
    var = jnp.mean(yc * yc, axis=-1, keepdims=True)
    return yc * lax.rsqrt(var + LN_EPS) * g + b


def _ln_kernel(x_ref, g_ref, b_ref, o_ref):
    o_ref[...] = _layer_norm_rows(x_ref[...], g_ref[...], b_ref[...])


def _layer_norm(x2d, g, b, tm=512):
    m, d = x2d.shape
    return pl.pallas_call(
        _ln_kernel,
        out_shape=jax.ShapeDtypeStruct((m, d), F32),
        grid=(m // tm,),
        in_specs=[pl.BlockSpec((tm, d), lambda i: (i, 0)),
                  pl.BlockSpec((1, d), lambda i: (0, 0)),
                  pl.BlockSpec((1, d), lambda i: (0, 0))],
        out_specs=pl.BlockSpec((tm, d), lambda i: (i, 0)),
        compiler_params=_cparams(("parallel",)),
        name="ln_in",
    )(x2d, g.reshape(1, d), b.reshape(1, d))


def _rope_block(t, c, s1, s2):
    return t * c + pltpu.roll(t, LANES - ROPE_DIM // 2, 1) * s1 + pltpu.roll(t, ROPE_DIM // 2, 1) * s2


def _in_proj_kernel(x_ref, w_ref, bf_ref, rc_ref, rs1_ref, rs2_ref,
                    fq_ref, fk_ref, fvt_ref, dq_ref, dkv_ref, mq_ref, mk_ref, mvt_ref, km_ref, carry_ref,
                    *, tm, tk, n_moba_blocks, tiles_per_seq):
    xb = x_ref[...].astype(BF16)
    rc, rs1, rs2 = rc_ref[...], rs1_ref[...], rs2_ref[...]

    def proj(seg):
        return _dot(xb, w_ref[:, seg[0]:seg[1]])

    def rope(t):
        return jnp.concatenate(
            [_rope_block(t[:, c:c + LANES], rc, rs1, rs2) for c in range(0, t.shape[1], LANES)], axis=1)

    def store_transposed(vt_ref, v):
        for p in range(v.shape[1] // LANES):
            vt = v[:, p * LANES:(p + 1) * LANES].T
            for c in range(tm // tk):
                vt_ref[0, p, c] = vt[:, c * tk:(c + 1) * tk].astype(BF16)

    fox = proj(SEG_FOX)
    store_transposed(fvt_ref, fox[:, 2 * WA:])
    z = proj(SEG_F) + bf_ref[...]
    flog = jnp.minimum(z, 0.0) - jnp.log1p(jnp.exp(-jnp.abs(z)))
    _fox_pack(flog, fox[:, :WA], fox[:, WA:2 * WA], fq_ref, fk_ref, carry_ref,
              pl.program_id(0) % tiles_per_seq == 0)

    dq_ref[...] = rope(proj(SEG_DQ)) * LOG2E
    dkv_ref[:, :WBK] = rope(proj(SEG_DK))
    dkv_ref[:, WBK:] = proj(SEG_DV)

    mq_ref[...] = (rope(proj(SEG_MQ)) * LOG2E).astype(BF16)
    store_transposed(mvt_ref, proj(SEG_MV))
    kc = rope(proj(SEG_MK))
    for r in range(tm // MOBA_BLOCK):
        km_ref[r] = jnp.sum(kc[r * MOBA_BLOCK:(r + 1) * MOBA_BLOCK], axis=0, keepdims=True) * (1.0 / MOBA_BLOCK)
    row = pl.program_id(0) * tm + lax.broadcasted_iota(jnp.int32, (tm, LANES), 0)
    blk = (row // MOBA_BLOCK) % n_moba_blocks
    lane = lax.broadcasted_iota(jnp.int32, (tm, LANES), 1)
    lo = lane < HEAD_DIM
    oh_hi = jnp.where(lane - HEAD_DIM == blk, 1.0, 0.0)
    oh_lo = jnp.where(lane == blk, 1.0, 0.0)
    for p in range(WC // LANES):
        kp = kc[:, p * LANES:(p + 1) * LANES]
        mk_ref[:, (2 * p) * LANES:(2 * p + 1) * LANES] = jnp.where(lo, kp, oh_hi).astype(BF16)
        mk_ref[:, (2 * p + 1) * LANES:(2 * p + 2) * LANES] = jnp.where(lo, oh_lo, kp).astype(BF16)


def _in_proj(x2d, w_cat, bf_pad, rope_tabs, bsz, seq, tm=512, tk=T_ATTN):
    m, d = x2d.shape
    n_rt = seq // tm
    n_moba_blocks = seq // MOBA_BLOCK
    assert n_moba_blocks <= HEAD_DIM and seq % tm == 0 and tm % MOBA_BLOCK == 0 and tm % tk == 0
    row = lambda i: (i, 0)
    tab = lambda i: (i % n_rt, 0)
    vt_shape = lambda w: jax.ShapeDtypeStruct((bsz, w // LANES, seq // tk, LANES, tk), BF16)
    vt_spec = lambda w: pl.BlockSpec((1, w // LANES, tm // tk, LANES, tk),
                                     lambda i: (i // n_rt, 0, i % n_rt, 0, 0))
    outs = (
        jax.ShapeDtypeStruct((m, FOX_HEADS * LANES), BF16),
        jax.ShapeDtypeStruct((m, FOX_HEADS * LANES), BF16),
        vt_shape(WA),
        jax.ShapeDtypeStruct((m, WBQ), F32),
        jax.ShapeDtypeStruct((m, 2 * WBK), F32),
        jax.ShapeDtypeStruct((m, WC), BF16),
        jax.ShapeDtypeStruct((m, 2 * WC), BF16),
        vt_shape(WC),
        jax.ShapeDtypeStruct((m // MOBA_BLOCK, 1, WC), F32),
    )
    return pl.pallas_call(
        functools.partial(_in_proj_kernel, tm=tm, tk=tk, n_moba_blocks=n_moba_blocks, tiles_per_seq=n_rt),
        out_shape=outs,
        grid=(m // tm,),
        in_specs=[pl.BlockSpec((tm, d), row),
                  pl.BlockSpec((d, N_PROJ), lambda i: (0, 0)),
                  pl.BlockSpec((1, LANES), lambda i: (0, 0)),
                  pl.BlockSpec((tm, LANES), tab),
                  pl.BlockSpec((tm, LANES), tab),
                  pl.BlockSpec((tm, LANES), tab)],
        out_specs=(pl.BlockSpec((tm, FOX_HEADS * LANES), row),
                   pl.BlockSpec((tm, FOX_HEADS * LANES), row),
                   vt_spec(WA),
                   pl.BlockSpec((tm, WBQ), row),
                   pl.BlockSpec((tm, 2 * WBK), row),
                   pl.BlockSpec((tm, WC), row),
                   pl.BlockSpec((tm, 2 * WC), row),
                   vt_spec(WC),
                   pl.BlockSpec((tm // MOBA_BLOCK, 1, WC), lambda i: (i, 0, 0))),
        scratch_shapes=[pltpu.VMEM((1, LANES), F32)],
        compiler_params=_cparams(("arbitrary",)),
        name="in_proj",
    )(x2d, w_cat, bf_pad, *rope_tabs)


def _fox_pack(flog, q, k, qa_ref, ka_ref, carry_ref, first_of_sequence):
    tc = flog.shape[0]

    @pl.when(first_of_sequence)
    def _():
        carry_ref[...] = jnp.zeros_like(carry_ref)

    r = lax.broadcasted_iota(jnp.int32, (tc, tc), 0)
    c = lax.broadcasted_iota(jnp.int32, (tc, tc), 1)
    tri = jnp.where(c <= r, 1.0, 0.0).astype(BF16)
    f_hi, f_mid, f_lo = _split3(flog)
    cs = _dot(tri, f_hi) + _dot(tri, f_mid) + _dot(tri, f_lo) + carry_ref[...]
    carry_ref[...] = cs[tc - 1:tc, :]

    lane = lax.broadcasted_iota(jnp.int32, (tc, LANES), 1)
    for h in range(FOX_HEADS):
        col = jnp.sum(jnp.where(lane == h, cs, 0.0), axis=1, keepdims=True) * LOG2E
        hi, mid, lo = [piece.astype(F32) for piece in _split3(col)]
        a = lane - HEAD_DIM * (1 - h % HEADS_PER_BLOCK)
        one = jnp.where((a >= 3) & (a < 6), 1.0, 0.0)
        aux_q = jnp.where(a == 0, hi, jnp.where(a == 1, mid, jnp.where(a == 2, lo, one)))
        one = jnp.where((a >= 0) & (a < 3), 1.0, 0.0)
        aux_k = jnp.where(a == 3, -hi, jnp.where(a == 4, -mid, jnp.where(a == 5, -lo, one)))
        own = lane // HEAD_DIM == h % HEADS_PER_BLOCK
        p = h // HEADS_PER_BLOCK
        qa_ref[:, h * LANES:(h + 1) * LANES] = jnp.where(own, q[:, p * LANES:(p + 1) * LANES] * LOG2E, aux_q).astype(BF16)
        ka_ref[:, h * LANES:(h + 1) * LANES] = jnp.where(own, k[:, p * LANES:(p + 1) * LANES], aux_k).astype(BF16)


def _attend(qs, k_refs, vt_ref, acc_refs, m_refs, i, t):
    key = lax.broadcasted_iota(jnp.int32, (t, t), 0)
    qry = lax.broadcasted_iota(jnp.int32, (t, t), 1)
    causal = key <= qry
    heads = range(HEADS_PER_BLOCK)
    def keys(h, j):
        return k_refs[h][0, pl.ds(pl.multiple_of(j * t, t), t), :]

    def values(vt, h):
        ones_rows = jnp.where(lax.broadcasted_iota(jnp.int32, (SUM_ROWS, vt.shape[1]), 0) == 0, 1.0, 0.0)
        return jnp.concatenate([vt[h * HEAD_DIM:(h + 1) * HEAD_DIM], ones_rows.astype(BF16)], axis=0)

    vt = vt_ref[0, 0, i]
    ss = [jnp.where(causal, _dot_nt(keys(h, i), qs[h]), -jnp.inf) for h in heads]
    ms = [jnp.max(s, axis=0, keepdims=True) for s in ss]
    ps = [jnp.exp2(s - m).astype(BF16) for s, m in zip(ss, ms)]
    for h in heads:
        m_refs[h][...] = ms[h]
        acc_refs[h][...] = _dot(values(vt, h), ps[h])

    def sweep(nc, first, last):
        def body(j):
            ms = [m_refs[h][...] for h in heads]
            ss = [_dot_nt(k_refs[h][0, pl.ds(pl.multiple_of(j * (nc * t), nc * t), nc * t), :], qs[h])
                  for h in heads]
            cmax = [jnp.max(ss[h], axis=0, keepdims=True) for h in heads]
            ok = jnp.maximum(jnp.max(cmax[0] - ms[0]), jnp.max(cmax[1] - ms[1])) <= EXP2_HEADROOM
            vt = jnp.concatenate([vt_ref[0, 0, j * nc + c] for c in range(nc)], axis=1)
            new = [acc_refs[h][...] + _dot(values(vt, h), jnp.exp2(ss[h] - ms[h]).astype(BF16)) for h in heads]

            @pl.when(ok)
            def _():
                for h in heads:
                    acc_refs[h][...] = new[h]

            @pl.when(jnp.logical_not(ok))
            def _():
                for h in heads:
                    m_new = jnp.maximum(ms[h], cmax[h])
                    acc_refs[h][...] = jnp.exp2(ms[h] - m_new) * acc_refs[h][...]
                    m_refs[h][...] = m_new

            return j + ok.astype(jnp.int32)

        lax.while_loop(lambda j: j < last, body, jnp.asarray(first, jnp.int32))

    done = 0
    for nc in CHUNK_GROUPS:
        sweep(nc, done // nc, i // nc)
        done = (i // nc) * nc
    o_t = jnp.concatenate([acc_refs[h][:HEAD_DIM] / acc_refs[h][HEAD_DIM:HEAD_DIM + 1] for h in heads], axis=0)
    return o_t.T


def _attn_scratch(t):
    return ([pltpu.VMEM((HEAD_DIM + SUM_ROWS, t), F32)] * HEADS_PER_BLOCK
            + [pltpu.VMEM((1, t), F32)] * HEADS_PER_BLOCK)


def _head_masks(shape):
    lane = lax.broadcasted_iota(jnp.int32, shape, 1)
    return [lane // HEAD_DIM == h for h in range(HEADS_PER_BLOCK)]


def _fox_kernel(q0_ref, q1_ref, k0_ref, k1_ref, vt_ref, o_ref, a0, a1, m0, m1, *, t):
    o = _attend((q0_ref[0], q1_ref[0]), (k0_ref, k1_ref), vt_ref, (a0, a1), (m0, m1), pl.program_id(2), t)
    o_ref[0] = o.astype(o_ref.dtype)


def _fox_attention(qa, ka, vt, t=T_ATTN):
    b, s, _ = qa.shape
    n_pairs = FOX_HEADS // HEADS_PER_BLOCK
    qspec = lambda h: pl.BlockSpec((1, t, LANES), lambda bi, p, i: (bi, i, HEADS_PER_BLOCK * p + h))
    kspec = lambda h: pl.BlockSpec((1, s, LANES), lambda bi, p, i: (bi, 0, HEADS_PER_BLOCK * p + h),
                                   pipeline_mode=pl.Buffered(1))
    return pl.pallas_call(
        functools.partial(_fox_kernel, t=t),
        out_shape=jax.ShapeDtypeStruct((b, s, WA), BF16),
        grid=(b, n_pairs, s // t),
        in_specs=[qspec(0), qspec(1), kspec(0), kspec(1),
                  pl.BlockSpec((1, 1, s // t, LANES, t), lambda bi, p, i: (bi, p, 0, 0, 0),
                               pipeline_mode=pl.Buffered(1))],
        out_specs=pl.BlockSpec((1, t, LANES), lambda bi, p, i: (bi, i, p)),
        scratch_shapes=_attn_scratch(t),
        compiler_params=_cparams(("parallel", "parallel", "arbitrary")),
        name="fox_attn",
    )(qa, qa, ka, ka, vt)


def _dil_band_bias(nk, nq, span, at_seq_start):
    kk = lax.broadcasted_iota(jnp.int32, (nk, nq), 0)
    qq = lax.broadcasted_iota(jnp.int32, (nk, nq), 1)
    dist = qq + span - kk
    valid = (dist >= 0) & (dist <= jnp.where(at_seq_start, jnp.minimum(qq, span), span))
    return jnp.where(valid, 0.0, -jnp.inf)


def _dil_bands(problems):
    heads = range(HEADS_PER_BLOCK)
    logits, vts = [], []
    for q, k, v, bias in problems:
        hm = _head_masks(q.shape)
        qb, kb = q.astype(BF16), k.astype(BF16)
        logits.append([_dot_nt(kb, jnp.where(hm[h], qb, jnp.zeros_like(qb))) + bias for h in heads])
        vts.append(v.T.astype(BF16))
    probs, lses, dens = [], [], []
    for ss in logits:
        ms = [jnp.max(s, axis=0, keepdims=True) for s in ss]
        ps = [jnp.exp2(s - m) for s, m in zip(ss, ms)]
        den = [jnp.sum(p, axis=0, keepdims=True) for p in ps]
        probs.append([p.astype(BF16) for p in ps])
        dens.append(den)
        lses.append([m + jnp.log2(d) for m, d in zip(ms, den)])
    outs = []
    for vt, ps, den, lse in zip(vts, probs, dens, lses):
        nq = ps[0].shape[1]
        o_t = [_dot(vt[h * HEAD_DIM:(h + 1) * HEAD_DIM], ps[h]) / den[h] for h in heads]
        e_t = [jnp.broadcast_to(lse[h], (HEAD_DIM, nq)) for h in heads]
        both = jnp.concatenate(o_t + e_t, axis=0).T
        outs.append((both[:, :LANES], both[:, LANES:]))
    return outs


def _dil_kernel(q0_ref, q1_ref, q2_ref, kc_ref, kp_ref, vc_ref, vp_ref, y_ref, o_scr, e_scr, *, tq, sub):
    at_start = pl.program_id(2) == 0
    q_refs = (q0_ref, q1_ref, q2_ref)
    problems, dests = [], []
    for g, (window, dil) in enumerate(DIL_PATTERNS):
        span = window // dil
        rows = tq // dil
        nq = min(sub, rows)
        bias = _dil_band_bias(nq + span, nq, span, False)
        bias_first = _dil_band_bias(nq + span, nq, span, at_start)

        def take(ref, r, first, n):
            idx = pl.ds(first * dil + r, n, stride=dil) if dil > 1 else pl.ds(first, n)
            return ref[0, idx, :]

        for r in range(dil):
            for u in range(rows // nq):
                if u == 0:
                    k = jnp.concatenate([take(kp_ref, r, rows - span, span), take(kc_ref, r, 0, nq)], axis=0)
                    v = jnp.concatenate([take(vp_ref, r, rows - span, span), take(vc_ref, r, 0, nq)], axis=0)
                else:
                    k = take(kc_ref, r, u * nq - span, nq + span)
                    v = take(vc_ref, r, u * nq - span, nq + span)
                problems.append((take(q_refs[g], r, u * nq, nq), k, v, bias_first if u == 0 else bias))
                dests.append((g, pl.ds(u * nq * dil + r, nq, stride=dil) if dil > 1 else pl.ds(u * nq, nq)))
                if len(problems) == DIL_BATCH:
                    for (gd, idx), (o, e) in zip(dests, _dil_bands(problems)):
                        o_scr[gd, idx, :] = o
                        e_scr[gd, idx, :] = e
                    problems, dests = [], []
    assert not problems
    e = [e_scr[g] for g in range(len(DIL_PATTERNS))]
    mx = functools.reduce(jnp.maximum, e)
    w = [jnp.exp2(eg - mx) for eg in e]
    num = functools.reduce(jnp.add, [w[g] * o_scr[g] for g in range(len(DIL_PATTERNS))])
    y_ref[0] = (num / functools.reduce(jnp.add, w)).astype(y_ref.dtype)


def _dilated_attention(dq, dkv, tq=T_DIL, sub=DIL_SUB):
    b, s, _ = dq.shape
    n_groups = len(DIL_PATTERNS)
    n_kv_pairs = DIL_KV_HEADS // HEADS_PER_BLOCK
    for window, dil in DIL_PATTERNS:
        span, rows = window // dil, tq // dil
        assert window % dil == 0 and tq % dil == 0 and span <= rows and rows % min(sub, rows) == 0
        assert span % 16 == 0 and min(sub, rows) % LANES == 0
    assert s % tq == 0
    cur = lambda col: (lambda bi, p, i: (bi, i, col + p))
    prv = lambda col: (lambda bi, p, i: (bi, jnp.maximum(i - 1, 0), col + p))
    blk = lambda imap: pl.BlockSpec((1, tq, LANES), imap)
    return pl.pallas_call(
        functools.partial(_dil_kernel, tq=tq, sub=sub),
        out_shape=jax.ShapeDtypeStruct((b, s, WBK), BF16),
        grid=(b, n_kv_pairs, s // tq),
        in_specs=[blk(cur(g * n_kv_pairs)) for g in range(n_groups)]
                 + [blk(cur(0)), blk(prv(0)), blk(cur(n_kv_pairs)), blk(prv(n_kv_pairs))],
        out_specs=blk(cur(0)),
        scratch_shapes=[pltpu.VMEM((n_groups, tq, LANES), F32)] * 2,
        compiler_params=_cparams(("parallel", "parallel", "arbitrary")),
        name="dilated_attn",
    )(*([dq] * n_groups), dkv, dkv, dkv, dkv)


def _moba_select_bias(gate, n_past):
    blk = lax.broadcasted_iota(jnp.int32, gate.shape, 0)
    g = jnp.where(blk < n_past, gate, -jnp.inf)
    sel = blk == n_past
    for _ in range(MOBA_TOPK):
        mx = jnp.max(g, axis=0, keepdims=True)
        is_max = (g == mx) & (mx > -jnp.inf)
        idx = jnp.min(jnp.where(is_max, blk, gate.shape[0]), axis=0, keepdims=True)
        pick = blk == idx
        sel = sel | pick
        g = jnp.where(pick, -jnp.inf, g)
    return jnp.where(sel, 0.0, NEG_BIG)


def _moba_kernel(q_ref, k0_ref, k1_ref, vt_ref, km_ref, o_ref, a0, a1, m0, m1, *, t):
    i = pl.program_id(2)
    q = q_ref[0]
    hm = _head_masks((t, LANES))
    km = km_ref[0]
    nb = km.shape[0]
    own_blk = (i * t + lax.broadcasted_iota(jnp.int32, (1, t), 1)) // MOBA_BLOCK
    k_hi, k_mid, k_lo = _split3(km)
    qaug = []
    for h in range(HEADS_PER_BLOCK):
        qh = jnp.where(hm[h], q, jnp.zeros_like(q))
        gate = _dot_nt(k_hi, qh) + _dot_nt(k_mid, qh) + _dot_nt(k_lo, qh)
        bias = _moba_select_bias(gate, own_blk)
        pads = (HEAD_DIM * (1 - h), LANES - HEAD_DIM * (1 - h) - nb)
        bias = jnp.concatenate([jnp.zeros((pads[0], t), F32)] * (pads[0] > 0) + [bias]
                               + [jnp.zeros((pads[1], t), F32)] * (pads[1] > 0), axis=0).T
        qaug.append(jnp.where(hm[h], q, bias.astype(BF16)))
    o = _attend(qaug, (k0_ref, k1_ref), vt_ref, (a0, a1), (m0, m1), i, t)
    o_ref[0] = o.astype(o_ref.dtype)


def _moba_attention(mq, mk, mvt, km, t=T_ATTN):
    b, s, _ = mq.shape
    assert t % MOBA_BLOCK == 0
    n_pairs = MOBA_HEADS // HEADS_PER_BLOCK
    nb = s // MOBA_BLOCK
    kspec = lambda h: pl.BlockSpec((1, s, LANES), lambda bi, p, i: (bi, 0, HEADS_PER_BLOCK * p + h),
                                   pipeline_mode=pl.Buffered(1))
    return pl.pallas_call(
        functools.partial(_moba_kernel, t=t),
        out_shape=jax.ShapeDtypeStruct((b, s, WC), BF16),
        grid=(b, n_pairs, s // t),
        in_specs=[pl.BlockSpec((1, t, LANES), lambda bi, p, i: (bi, i, p)),
                  kspec(0), kspec(1),
                  pl.BlockSpec((1, 1, s // t, LANES, t), lambda bi, p, i: (bi, p, 0, 0, 0),
                               pipeline_mode=pl.Buffered(1)),
                  pl.BlockSpec((1, nb, LANES), lambda bi, p, i: (bi, 0, p))],
        out_specs=pl.BlockSpec((1, t, LANES), lambda bi, p, i: (bi, i, p)),
        scratch_shapes=_attn_scratch(t),
        compiler_params=_cparams(("parallel", "parallel", "arbitrary")),
        name="moba_attn",
    )(mq, mk, mk, mvt, km)


def _sigmoid(z):
    return 1.0 / (1.0 + jnp.exp(-z))


def _merge_kernel(x_ref, ya_ref, yb_ref, yc_ref,
                  wg_ref, wpa_ref, wpb_ref, wpc_ref, wo_ref, g_ref, b_ref, out_ref):
    x = x_ref[...]
    xb = x.astype(BF16)
    merged = (_sigmoid(_dot(xb, wg_ref[:, :D_MODEL])) * _dot(ya_ref[...], wpa_ref[...])
              + _sigmoid(_dot(xb, wg_ref[:, D_MODEL:2 * D_MODEL])) * _dot(yb_ref[...], wpb_ref[...])
              + _sigmoid(_dot(xb, wg_ref[:, 2 * D_MODEL:])) * _dot(yc_ref[...], wpc_ref[...]))
    mix = _dot(merged.astype(BF16), wo_ref[...])
    out_ref[...] = _layer_norm_rows(DN_ALPHA * x + mix, g_ref[...], b_ref[...])


def _merge(x2d, ya, yb, yc, wg, wpa, wpb, wpc, wo, g, b, tm=512):
    m, d = x2d.shape
    row = lambda i: (i, 0)
    full = lambda i: (0, 0)
    rows = lambda w: pl.BlockSpec((tm, w), row)
    whole = lambda a: pl.BlockSpec(a.shape, full, pipeline_mode=pl.Buffered(1))
    return pl.pallas_call(
        _merge_kernel,
        out_shape=jax.ShapeDtypeStruct((m, d), F32),
        grid=(m // tm,),
        in_specs=[rows(d), rows(WA), rows(WBK), rows(WC),
                  whole(wg), whole(wpa), whole(wpb), whole(wpc), whole(wo),
                  pl.BlockSpec((1, d), full), pl.BlockSpec((1, d), full)],
        out_specs=rows(d),
        compiler_params=_cparams(("parallel",)),
        name="merge_proj_ln",
    )(x2d, ya, yb, yc, wg, wpa, wpb, wpc, wo, g.reshape(1, d), b.reshape(1, d))


def _ffn_kernel(x_ref, wg_ref, wu_ref, wd_ref, g_ref, b_ref, out_ref):
    x = x_ref[...]
    xb = x.astype(BF16)
    hg = _dot(xb, wg_ref[...])
    hid = hg * _sigmoid(hg) * _dot(xb, wu_ref[...])
    ffn = _dot(hid.astype(BF16), wd_ref[...])
    out_ref[...] = _layer_norm_rows(DN_ALPHA * x + ffn, g_ref[...], b_ref[...])


def _ffn(x2d, wg, wu, wd, g, b, tm=512):
    m, d = x2d.shape
    full = lambda i: (0, 0)
    whole = lambda a: pl.BlockSpec(a.shape, full, pipeline_mode=pl.Buffered(1))
    return pl.pallas_call(
        _ffn_kernel,
        out_shape=jax.ShapeDtypeStruct((m, d), F32),
        grid=(m // tm,),
        in_specs=[pl.BlockSpec((tm, d), lambda i: (i, 0)),
                  whole(wg), whole(wu), whole(wd),
                  pl.BlockSpec((1, d), full), pl.BlockSpec((1, d), full)],
        out_specs=pl.BlockSpec((tm, d), lambda i: (i, 0)),
        compiler_params=_cparams(("parallel",)),
        name="swiglu_ln",
    )(x2d, wg, wu, wd, g.reshape(1, d), b.reshape(1, d))


def _rope_tables(seq):
    half = ROPE_DIM // 2
    inv_freq = jnp.power(ROPE_THETA, -jnp.arange(half, dtype=F32) * 2.0 / ROPE_DIM)
    dim = np.arange(LANES) % HEAD_DIM
    ang = jnp.arange(seq).astype(F32)[:, None] * inv_freq[dim % half][None, :]
    cos, sin = jnp.cos(ang), jnp.sin(ang)
    first, second = (dim < half)[None, :], ((dim >= half) & (dim < ROPE_DIM))[None, :]
    return (jnp.where(first | second, cos, 1.0), jnp.where(first, -sin, 0.0), jnp.where(second, sin, 0.0))


def _pack_w_in(w, b_f):
    sizes = (WA, WA, WA, FOX_HEADS, WBQ, WBK, WBK, WC, WC, WC, N_GATE)
    offs = np.cumsum(sizes)[:-1].tolist()
    qa, ka, va, fa, qb, kb, vb, qc, kc, vc, g = jnp.split(w, offs, axis=1)
    fa_pad = jnp.pad(fa, ((0, 0), (0, LANES - FOX_HEADS)))
    w_cat = jnp.concatenate([qa * SCALE, ka, va, fa_pad, qb * SCALE, kb, vb, qc * SCALE, kc, vc], axis=1)
    bf_pad = jnp.pad(b_f, (0, LANES - FOX_HEADS)).reshape(1, LANES)
    return w_cat.astype(BF16), g.astype(BF16), bf_pad


def kernel(x, ln_in_g, ln_in_b, w_in, b_f, w_pa, w_pb, w_pc, w_o, ln1_g, ln1_b,
           w_gate, w_up, w_down, ln2_g, ln2_b):
    bsz, seq, d = x.shape
    m = bsz * seq
    rope_tabs = _rope_tables(seq)
    h = _layer_norm(x.reshape(m, d), ln_in_g, ln_in_b)
    for l in range(DEPTH):
        w_cat, w_g, bf_pad = _pack_w_in(w_in[l], b_f[l])
        qa, ka, fvt, dq, dkv, mq, mk, mvt, km = _in_proj(h, w_cat, bf_pad, rope_tabs, bsz, seq)
        shp = lambda a: a.reshape(bsz, seq, a.shape[-1])
        ya = _fox_attention(shp(qa), shp(ka), fvt)
        yb = _dilated_attention(shp(dq), shp(dkv))
        yc = _moba_attention(shp(mq), shp(mk), mvt, km.reshape(bsz, seq // MOBA_BLOCK, WC))
        h = _merge(h, ya.reshape(m, WA), yb.reshape(m, WBK), yc.reshape(m, WC),
                   w_g, w_pa[l].astype(BF16), w_pb[l].astype(BF16), w_pc[l].astype(BF16),
                   w_o[l].astype(BF16), ln1_g[l], ln1_b[l])
        h = _ffn(h, w_gate[l].astype(BF16), w_up[l].astype(BF16), w_down[l].astype(BF16),
                 ln2_g[l], ln2_b[l])
    return h.reshape(bsz, seq, d)
```

```python
import functools

import jax
import jax.numpy as jnp
import numpy as np
from jax import lax
from jax.experimental import pallas as pl
from jax.experimental.pallas import tpu as pltpu

D_MODEL = 1024
DEPTH = 2
HEAD_DIM = 64
FOX_HEADS = 6
DIL_KV_HEADS = 4
DIL_PATTERNS = ((128, 1), (512, 4), (2048, 16))
DIL_Q_HEADS = DIL_KV_HEADS * len(DIL_PATTERNS)
MOBA_HEADS = 6
MOBA_BLOCK = 256
MOBA_TOPK = 3
ROPE_THETA = 500000.0
ROPE_DIM = HEAD_DIM // 4
N_BRANCH = 3
D_FF = -(-(8 * D_MODEL) // (3 * 256)) * 256
LN_EPS = 1e-5
DN_ALPHA = (2 * DEPTH) ** 0.25
SCALE = HEAD_DIM ** -0.5

LANES = 128
HEADS_PER_BLOCK = LANES // HEAD_DIM
VMEM_LIMIT = 48 * 1024 * 1024

WA = FOX_HEADS * HEAD_DIM
WBQ = DIL_Q_HEADS * HEAD_DIM
WBK = DIL_KV_HEADS * HEAD_DIM
WC = MOBA_HEADS * HEAD_DIM
N_GATE = N_BRANCH * D_MODEL
SEG_FOX = (0, 3 * WA)
SEG_F = (SEG_FOX[1], SEG_FOX[1] + LANES)
SEG_DQ = (SEG_F[1], SEG_F[1] + WBQ)
SEG_DK = (SEG_DQ[1], SEG_DQ[1] + WBK)
SEG_DV = (SEG_DK[1], SEG_DK[1] + WBK)
SEG_MQ = (SEG_DV[1], SEG_DV[1] + WC)
SEG_MK = (SEG_MQ[1], SEG_MQ[1] + WC)
SEG_MV = (SEG_MK[1], SEG_MK[1] + WC)
N_PROJ = SEG_MV[1]

T_ATTN = 2 * MOBA_BLOCK
T_DIL = 2048
DIL_SUB = 128
DIL_BATCH = 8
NEG_BIG = -1e30
LOG2E = 1.4426950408889634
EXP2_HEADROOM = 64.0
SUM_ROWS = 16
CHUNK_GROUPS = (8, 4, 2, 1)
BF16 = jnp.bfloat16
F32 = jnp.float32


def _cparams(sem):
    return pltpu.CompilerParams(dimension_semantics=sem, vmem_limit_bytes=VMEM_LIMIT)


def _dot(a, b):
    return jnp.dot(a, b, preferred_element_type=F32)


def _dot_nt(a, b):
    return lax.dot_general(a, b, (((1,), (1,)), ((), ())), preferred_element_type=F32)


def _split3(x):
    hi = x.astype(BF16)
    r1 = x - hi.astype(F32)
    mid = r1.astype(BF16)
    lo = (r1 - mid.astype(F32)).astype(BF16)
    return hi, mid, lo


def _layer_norm_rows(y, g, b):
    mu = jnp.mean(y, axis=-1, keepdims=True)
    yc = y - mu
    var = jnp.mean(yc * yc, axis=-1, keepdims=True)
    return yc * lax.rsqrt(var + LN_EPS) * g + b


def _ln_kernel(x_ref, g_ref, b_ref, o_ref):
    o_ref[...] = _layer_norm_rows(x_ref[...], g_ref[...], b_ref[...])


def _layer_norm(x2d, g, b, tm=512):
    m, d = x2d.shape
    return pl.pallas_call(
        _ln_kernel,
        out_shape=jax.ShapeDtypeStruct((m, d), F32),
        grid=(m // tm,),
        in_specs=[pl.BlockSpec((tm, d), lambda i: (i, 0)),
                  pl.BlockSpec((1, d), lambda i: (0, 0)),
                  pl.BlockSpec((1, d), lambda i: (0, 0))],
        out_specs=pl.BlockSpec((tm, d), lambda i: (i, 0)),
        compiler_params=_cparams(("parallel",)),
        name="ln_in",
    )(x2d, g.reshape(1, d), b.reshape(1, d))


def _rope_block(t, c, s1, s2):
    return t * c + pltpu.roll(t, LANES - ROPE_DIM // 2, 1) * s1 + pltpu.roll(t, ROPE_DIM // 2, 1) * s2


def _in_proj_kernel(x_ref, w_ref, bf_ref, rc_ref, rs1_ref, rs2_ref,
                    fq_ref, fk_ref, fvt_ref, dq_ref, dkv_ref, mq_ref, mk_ref, mvt_ref, km_ref, carry_ref,
                    *, tm, tk, n_moba_blocks, tiles_per_seq):
    xb = x_ref[...].astype(BF16)
    rc, rs1, rs2 = rc_ref[...], rs1_ref[...], rs2_ref[...]

    def proj(seg):
        return _dot(xb, w_ref[:, seg[0]:seg[1]])

    def rope(t):
        return jnp.concatenate(
            [_rope_block(t[:, c:c + LANES], rc, rs1, rs2) for c in range(0, t.shape[1], LANES)], axis=1)

    def store_transposed(vt_ref, v):
        for p in range(v.shape[1] // LANES):
            vt = v[:, p * LANES:(p + 1) * LANES].T
            for c in range(tm // tk):
                vt_ref[0, p, c] = vt[:, c * tk:(c + 1) * tk].astype(BF16)

    fox = proj(SEG_FOX)
    store_transposed(fvt_ref, fox[:, 2 * WA:])
    z = proj(SEG_F) + bf_ref[...]
    flog = jnp.minimum(z, 0.0) - jnp.log1p(jnp.exp(-jnp.abs(z)))
    _fox_pack(flog, fox[:, :WA], fox[:, WA:2 * WA], fq_ref, fk_ref, carry_ref,
              pl.program_id(0) % tiles_per_seq == 0)

    dq_ref[...] = rope(proj(SEG_DQ)) * LOG2E
    dkv_ref[:, :WBK] = rope(proj(SEG_DK))
    dkv_ref[:, WBK:] = proj(SEG_DV)

    mq_ref[...] = (rope(proj(SEG_MQ)) * LOG2E).astype(BF16)
    store_transposed(mvt_ref, proj(SEG_MV))
    kc = rope(proj(SEG_MK))
    for r in range(tm // MOBA_BLOCK):
        km_ref[r] = jnp.sum(kc[r * MOBA_BLOCK:(r + 1) * MOBA_BLOCK], axis=0, keepdims=True) * (1.0 / MOBA_BLOCK)
    row = pl.program_id(0) * tm + lax.broadcasted_iota(jnp.int32, (tm, LANES), 0)
    blk = (row // MOBA_BLOCK) % n_moba_blocks
    lane = lax.broadcasted_iota(jnp.int32, (tm, LANES), 1)
    lo = lane < HEAD_DIM
    oh_hi = jnp.where(lane - HEAD_DIM == blk, 1.0, 0.0)
    oh_lo = jnp.where(lane == blk, 1.0, 0.0)
    for p in range(WC // LANES):
        kp = kc[:, p * LANES:(p + 1) * LANES]
        mk_ref[:, (2 * p) * LANES:(2 * p + 1) * LANES] = jnp.where(lo, kp, oh_hi).astype(BF16)
        mk_ref[:, (2 * p + 1) * LANES:(2 * p + 2) * LANES] = jnp.where(lo, oh_lo, kp).astype(BF16)


def _in_proj(x2d, w_cat, bf_pad, rope_tabs, bsz, seq, tm=512, tk=T_ATTN):
    m, d = x2d.shape
    n_rt = seq // tm
    n_moba_blocks = seq // MOBA_BLOCK
    assert n_moba_blocks <= HEAD_DIM and seq % tm == 0 and tm % MOBA_BLOCK == 0 and tm % tk == 0
    row = lambda i: (i, 0)
    tab = lambda i: (i % n_rt, 0)
    vt_shape = lambda w: jax.ShapeDtypeStruct((bsz, w // LANES, seq // tk, LANES, tk), BF16)
    vt_spec = lambda w: pl.BlockSpec((1, w // LANES, tm // tk, LANES, tk),
                                     lambda i: (i // n_rt, 0, i % n_rt, 0, 0))
    outs = (
        jax.ShapeDtypeStruct((m, FOX_HEADS * LANES), BF16),
        jax.ShapeDtypeStruct((m, FOX_HEADS * LANES), BF16),
        vt_shape(WA),
        jax.ShapeDtypeStruct((m, WBQ), F32),
        jax.ShapeDtypeStruct((m, 2 * WBK), F32),
        jax.ShapeDtypeStruct((m, WC), BF16),
        jax.ShapeDtypeStruct((m, 2 * WC), BF16),
        vt_shape(WC),
        jax.ShapeDtypeStruct((m // MOBA_BLOCK, 1, WC), F32),
    )
    return pl.pallas_call(
        functools.partial(_in_proj_kernel, tm=tm, tk=tk, n_moba_blocks=n_moba_blocks, tiles_per_seq=n_rt),
        out_shape=outs,
        grid=(m // tm,),
        in_specs=[pl.BlockSpec((tm, d), row),
                  pl.BlockSpec((d, N_PROJ), lambda i: (0, 0)),
                  pl.BlockSpec((1, LANES), lambda i: (0, 0)),
                  pl.BlockSpec((tm, LANES), tab),
                  pl.BlockSpec((tm, LANES), tab),
                  pl.BlockSpec((tm, LANES), tab)],
        out_specs=(pl.BlockSpec((tm, FOX_HEADS * LANES), row),
                   pl.BlockSpec((tm, FOX_HEADS * LANES), row),
                   vt_spec(WA),
                   pl.BlockSpec((tm, WBQ), row),
                   pl.BlockSpec((tm, 2 * WBK), row),
                   pl.BlockSpec((tm, WC), row),
                   pl.BlockSpec((tm, 2 * WC), row),
                   vt_spec(WC),
                   pl.BlockSpec((tm // MOBA_BLOCK, 1, WC), lambda i: (i, 0, 0))),
        scratch_shapes=[pltpu.VMEM((1, LANES), F32)],
        compiler_params=_cparams(("arbitrary",)),
        name="in_proj",
    )(x2d, w_cat, bf_pad, *rope_tabs)


def _fox_pack(flog, q, k, qa_ref, ka_ref, carry_ref, first_of_sequence):
    tc = flog.shape[0]

    @pl.when(first_of_sequence)
    def _():
        carry_ref[...] = jnp.zeros_like(carry_ref)

    r = lax.broadcasted_iota(jnp.int32, (tc, tc), 0)
    c = lax.broadcasted_iota(jnp.int32, (tc, tc), 1)
    tri = jnp.where(c <= r, 1.0, 0.0).astype(BF16)
    f_hi, f_mid, f_lo = _split3(flog)
    cs = _dot(tri, f_hi) + _dot(tri, f_mid) + _dot(tri, f_lo) + carry_ref[...]
    carry_ref[...] = cs[tc - 1:tc, :]

    lane = lax.broadcasted_iota(jnp.int32, (tc, LANES), 1)
    for h in range(FOX_HEADS):
        col = jnp.sum(jnp.where(lane == h, cs, 0.0), axis=1, keepdims=True) * LOG2E
        hi, mid, lo = [piece.astype(F32) for piece in _split3(col)]
        a = lane - HEAD_DIM * (1 - h % HEADS_PER_BLOCK)
        one = jnp.where((a >= 3) & (a < 6), 1.0, 0.0)
        aux_q = jnp.where(a == 0, hi, jnp.where(a == 1, mid, jnp.where(a == 2, lo, one)))
        one = jnp.where((a >= 0) & (a < 3), 1.0, 0.0)
        aux_k = jnp.where(a == 3, -hi, jnp.where(a == 4, -mid, jnp.where(a == 5, -lo, one)))
        own = lane // HEAD_DIM == h % HEADS_PER_BLOCK
        p = h // HEADS_PER_BLOCK
        qa_ref[:, h * LANES:(h + 1) * LANES] = jnp.where(own, q[:, p * LANES:(p + 1) * LANES] * LOG2E, aux_q).astype(BF16)
        ka_ref[:, h * LANES:(h + 1) * LANES] = jnp.where(own, k[:, p * LANES:(p + 1) * LANES], aux_k).astype(BF16)


def _attend(qs, k_refs, vt_ref, acc_refs, m_refs, i, t):
    key = lax.broadcasted_iota(jnp.int32, (t, t), 0)
    qry = lax.broadcasted_iota(jnp.int32, (t, t), 1)
    causal = key <= qry
    heads = range(HEADS_PER_BLOCK)
    def keys(h, j):
        return k_refs[h][0, pl.ds(pl.multiple_of(j * t, t), t), :]

    def values(vt, h):
        ones_rows = jnp.where(lax.broadcasted_iota(jnp.int32, (SUM_ROWS, vt.shape[1]), 0) == 0, 1.0, 0.0)
        return jnp.concatenate([vt[h * HEAD_DIM:(h + 1) * HEAD_DIM], ones_rows.astype(BF16)], axis=0)

    vt = vt_ref[0, 0, i]
    ss = [jnp.where(causal, _dot_nt(keys(h, i), qs[h]), -jnp.inf) for h in heads]
    ms = [jnp.max(s, axis=0, keepdims=True) for s in ss]
    ps = [jnp.exp2(s - m).astype(BF16) for s, m in zip(ss, ms)]
    for h in heads:
        m_refs[h][...] = ms[h]
        acc_refs[h][...] = _dot(values(vt, h), ps[h])

    def sweep(nc, first, last):
        def body(j):
            ms = [m_refs[h][...] for h in heads]
            ss = [_dot_nt(k_refs[h][0, pl.ds(pl.multiple_of(j * (nc * t), nc * t), nc * t), :], qs[h])
                  for h in heads]
            cmax = [jnp.max(ss[h], axis=0, keepdims=True) for h in heads]
            ok = jnp.maximum(jnp.max(cmax[0] - ms[0]), jnp.max(cmax[1] - ms[1])) <= EXP2_HEADROOM
            vt = jnp.concatenate([vt_ref[0, 0, j * nc + c] for c in range(nc)], axis=1)
            new = [acc_refs[h][...] + _dot(values(vt, h), jnp.exp2(ss[h] - ms[h]).astype(BF16)) for h in heads]

            @pl.when(ok)
            def _():
                for h in heads:
                    acc_refs[h][...] = new[h]

            @pl.when(jnp.logical_not(ok))
            def _():
                for h in heads:
                    m_new = jnp.maximum(ms[h], cmax[h])
                    acc_refs[h][...] = jnp.exp2(ms[h] - m_new) * acc_refs[h][...]
                    m_refs[h][...] = m_new

            return j + ok.astype(jnp.int32)

        lax.while_loop(lambda j: j < last, body, jnp.asarray(first, jnp.int32))

    done = 0
    for nc in CHUNK_GROUPS:
        sweep(nc, done // nc, i // nc)
        done = (i // nc) * nc
    o_t = jnp.concatenate([acc_refs[h][:HEAD_DIM] / acc_refs[h][HEAD_DIM:HEAD_DIM + 1] for h in heads], axis=0)
    return o_t.T


def _attn_scratch(t):
    return ([pltpu.VMEM((HEAD_DIM + SUM_ROWS, t), F32)] * HEADS_PER_BLOCK
            + [pltpu.VMEM((1, t), F32)] * HEADS_PER_BLOCK)


def _head_masks(shape):
    lane = lax.broadcasted_iota(jnp.int32, shape, 1)
    return [lane // HEAD_DIM == h for h in range(HEADS_PER_BLOCK)]


def _fox_kernel(q0_ref, q1_ref, k0_ref, k1_ref, vt_ref, o_ref, a0, a1, m0, m1, *, t):
    o = _attend((q0_ref[0], q1_ref[0]), (k0_ref, k1_ref), vt_ref, (a0, a1), (m0, m1), pl.program_id(2), t)
    o_ref[0] = o.astype(o_ref.dtype)


def _fox_attention(qa, ka, vt, t=T_ATTN):
    b, s, _ = qa.shape
    n_pairs = FOX_HEADS // HEADS_PER_BLOCK
    qspec = lambda h: pl.BlockSpec((1, t, LANES), lambda bi, p, i: (bi, i, HEADS_PER_BLOCK * p + h))
    kspec = lambda h: pl.BlockSpec((1, s, LANES), lambda bi, p, i: (bi, 0, HEADS_PER_BLOCK * p + h),
                                   pipeline_mode=pl.Buffered(1))
    return pl.pallas_call(
        functools.partial(_fox_kernel, t=t),
        out_shape=jax.ShapeDtypeStruct((b, s, WA), BF16),
        grid=(b, n_pairs, s // t),
        in_specs=[qspec(0), qspec(1), kspec(0), kspec(1),
                  pl.BlockSpec((1, 1, s // t, LANES, t), lambda bi, p, i: (bi, p, 0, 0, 0),
                               pipeline_mode=pl.Buffered(1))],
        out_specs=pl.BlockSpec((1, t, LANES), lambda bi, p, i: (bi, i, p)),
        scratch_shapes=_attn_scratch(t),
        compiler_params=_cparams(("parallel", "parallel", "arbitrary")),
        name="fox_attn",
    )(qa, qa, ka, ka, vt)


def _dil_band_bias(nk, nq, span, at_seq_start):
    kk = lax.broadcasted_iota(jnp.int32, (nk, nq), 0)
    qq = lax.broadcasted_iota(jnp.int32, (nk, nq), 1)
    dist = qq + span - kk
    valid = (dist >= 0) & (dist <= jnp.where(at_seq_start, jnp.minimum(qq, span), span))
    return jnp.where(valid, 0.0, -jnp.inf)


def _dil_bands(problems):
    heads = range(HEADS_PER_BLOCK)
    logits, vts = [], []
    for q, k, v, bias in problems:
        hm = _head_masks(q.shape)
        qb, kb = q.astype(BF16), k.astype(BF16)
        logits.append([_dot_nt(kb, jnp.where(hm[h], qb, jnp.zeros_like(qb))) + bias for h in heads])
        vts.append(v.T.astype(BF16))
    probs, lses, dens = [], [], []
    for ss in logits:
        ms = [jnp.max(s, axis=0, keepdims=True) for s in ss]
        ps = [jnp.exp2(s - m) for s, m in zip(ss, ms)]
        den = [jnp.sum(p, axis=0, keepdims=True) for p in ps]
        probs.append([p.astype(BF16) for p in ps])
        dens.append(den)
        lses.append([m + jnp.log2(d) for m, d in zip(ms, den)])
    outs = []
    for vt, ps, den, lse in zip(vts, probs, dens, lses):
        nq = ps[0].shape[1]
        o_t = [_dot(vt[h * HEAD_DIM:(h + 1) * HEAD_DIM], ps[h]) / den[h] for h in heads]
        e_t = [jnp.broadcast_to(lse[h], (HEAD_DIM, nq)) for h in heads]
        both = jnp.concatenate(o_t + e_t, axis=0).T
        outs.append((both[:, :LANES], both[:, LANES:]))
    return outs


def _dil_kernel(q0_ref, q1_ref, q2_ref, kc_ref, kp_ref, vc_ref, vp_ref, y_ref, o_scr, e_scr, *, tq, sub):
    at_start = pl.program_id(2) == 0
    q_refs = (q0_ref, q1_ref, q2_ref)
    problems, dests = [], []
    for g, (window, dil) in enumerate(DIL_PATTERNS):
        span = window // dil
        rows = tq // dil
        nq = min(sub, rows)
        bias = _dil_band_bias(nq + span, nq, span, False)
        bias_first = _dil_band_bias(nq + span, nq, span, at_start)

        def take(ref, r, first, n):
            idx = pl.ds(first * dil + r, n, stride=dil) if dil > 1 else pl.ds(first, n)
            return ref[0, idx, :]

        for r in range(dil):
            for u in range(rows // nq):
                if u == 0:
                    k = jnp.concatenate([take(kp_ref, r, rows - span, span), take(kc_ref, r, 0, nq)], axis=0)
                    v = jnp.concatenate([take(vp_ref, r, rows - span, span), take(vc_ref, r, 0, nq)], axis=0)
                else:
                    k = take(kc_ref, r, u * nq - span, nq + span)
                    v = take(vc_ref, r, u * nq - span, nq + span)
                problems.append((take(q_refs[g], r, u * nq, nq), k, v, bias_first if u == 0 else bias))
                dests.append((g, pl.ds(u * nq * dil + r, nq, stride=dil) if dil > 1 else pl.ds(u * nq, nq)))
                if len(problems) == DIL_BATCH:
                    for (gd, idx), (o, e) in zip(dests, _dil_bands(problems)):
                        o_scr[gd, idx, :] = o
                        e_scr[gd, idx, :] = e
                    problems, dests = [], []
    assert not problems
    e = [e_scr[g] for g in range(len(DIL_PATTERNS))]
    mx = functools.reduce(jnp.maximum, e)
    w = [jnp.exp2(eg - mx) for eg in e]
    num = functools.reduce(jnp.add, [w[g] * o_scr[g] for g in range(len(DIL_PATTERNS))])
    y_ref[0] = (num / functools.reduce(jnp.add, w)).astype(y_ref.dtype)


def _dilated_attention(dq, dkv, tq=T_DIL, sub=DIL_SUB):
    b, s, _ = dq.shape
    n_groups = len(DIL_PATTERNS)
    n_kv_pairs = DIL_KV_HEADS // HEADS_PER_BLOCK
    for window, dil in DIL_PATTERNS:
        span, rows = window // dil, tq // dil
        assert window % dil == 0 and tq % dil == 0 and span <= rows and rows % min(sub, rows) == 0
        assert span % 16 == 0 and min(sub, rows) % LANES == 0
    assert s % tq == 0
    cur = lambda col: (lambda bi, p, i: (bi, i, col + p))
    prv = lambda col: (lambda bi, p, i: (bi, jnp.maximum(i - 1, 0), col + p))
    blk = lambda imap: pl.BlockSpec((1, tq, LANES), imap)
    return pl.pallas_call(
        functools.partial(_dil_kernel, tq=tq, sub=sub),
        out_shape=jax.ShapeDtypeStruct((b, s, WBK), BF16),
        grid=(b, n_kv_pairs, s // tq),
        in_specs=[blk(cur(g * n_kv_pairs)) for g in range(n_groups)]
                 + [blk(cur(0)), blk(prv(0)), blk(cur(n_kv_pairs)), blk(prv(n_kv_pairs))],
        out_specs=blk(cur(0)),
        scratch_shapes=[pltpu.VMEM((n_groups, tq, LANES), F32)] * 2,
        compiler_params=_cparams(("parallel", "parallel", "arbitrary")),
        name="dilated_attn",
    )(*([dq] * n_groups), dkv, dkv, dkv, dkv)


def _moba_select_bias(gate, n_past):
    blk = lax.broadcasted_iota(jnp.int32, gate.shape, 0)
    g = jnp.where(blk < n_past, gate, -jnp.inf)
    sel = blk == n_past
    for _ in range(MOBA_TOPK):
        mx = jnp.max(g, axis=0, keepdims=True)
        is_max = (g == mx) & (mx > -jnp.inf)
        idx = jnp.min(jnp.where(is_max, blk, gate.shape[0]), axis=0, keepdims=True)
        pick = blk == idx
        sel = sel | pick
        g = jnp.where(pick, -jnp.inf, g)
    return jnp.where(sel, 0.0, NEG_BIG)


def _moba_kernel(q_ref, k0_ref, k1_ref, vt_ref, km_ref, o_ref, a0, a1, m0, m1, *, t):
    i = pl.program_id(2)
    q = q_ref[0]
    hm = _head_masks((t, LANES))
    km = km_ref[0]
    nb = km.shape[0]
    own_blk = (i * t + lax.broadcasted_iota(jnp.int32, (1, t), 1)) // MOBA_BLOCK
    k_hi, k_mid, k_lo = _split3(km)
    qaug = []
    for h in range(HEADS_PER_BLOCK):
        qh = jnp.where(hm[h], q, jnp.zeros_like(q))
        gate = _dot_nt(k_hi, qh) + _dot_nt(k_mid, qh) + _dot_nt(k_lo, qh)
        bias = _moba_select_bias(gate, own_blk)
        pads = (HEAD_DIM * (1 - h), LANES - HEAD_DIM * (1 - h) - nb)
        bias = jnp.concatenate([jnp.zeros((pads[0], t), F32)] * (pads[0] > 0) + [bias]
                               + [jnp.zeros((pads[1], t), F32)] * (pads[1] > 0), axis=0).T
        qaug.append(jnp.where(hm[h], q, bias.astype(BF16)))
    o = _attend(qaug, (k0_ref, k1_ref), vt_ref, (a0, a1), (m0, m1), i, t)
    o_ref[0] = o.astype(o_ref.dtype)


def _moba_attention(mq, mk, mvt, km, t=T_ATTN):
    b, s, _ = mq.shape
    assert t % MOBA_BLOCK == 0
    n_pairs = MOBA_HEADS // HEADS_PER_BLOCK
    nb = s // MOBA_BLOCK
    kspec = lambda h: pl.BlockSpec((1, s, LANES), lambda bi, p, i: (bi, 0, HEADS_PER_BLOCK * p + h),
                                   pipeline_mode=pl.Buffered(1))
    return pl.pallas_call(
        functools.partial(_moba_kernel, t=t),
        out_shape=jax.ShapeDtypeStruct((b, s, WC), BF16),
        grid=(b, n_pairs, s // t),
        in_specs=[pl.BlockSpec((1, t, LANES), lambda bi, p, i: (bi, i, p)),
                  kspec(0), kspec(1),
                  pl.BlockSpec((1, 1, s // t, LANES, t), lambda bi, p, i: (bi, p, 0, 0, 0),
                               pipeline_mode=pl.Buffered(1)),
                  pl.BlockSpec((1, nb, LANES), lambda bi, p, i: (bi, 0, p))],
        out_specs=pl.BlockSpec((1, t, LANES), lambda bi, p, i: (bi, i, p)),
        scratch_shapes=_attn_scratch(t),
        compiler_params=_cparams(("parallel", "parallel", "arbitrary")),
        name="moba_attn",
    )(mq, mk, mk, mvt, km)


def _sigmoid(z):
    return 1.0 / (1.0 + jnp.exp(-z))


def _merge_kernel(x_ref, ya_ref, yb_ref, yc_ref,
                  wg_ref, wpa_ref, wpb_ref, wpc_ref, wo_ref, g_ref, b_ref, out_ref):
    x = x_ref[...]
    xb = x.astype(BF16)
    merged = (_sigmoid(_dot(xb, wg_ref[:, :D_MODEL])) * _dot(ya_ref[...], wpa_ref[...])
              + _sigmoid(_dot(xb, wg_ref[:, D_MODEL:2 * D_MODEL])) * _dot(yb_ref[...], wpb_ref[...])
              + _sigmoid(_dot(xb, wg_ref[:, 2 * D_MODEL:])) * _dot(yc_ref[...], wpc_ref[...]))
    mix = _dot(merged.astype(BF16), wo_ref[...])
    out_ref[...] = _layer_norm_rows(DN_ALPHA * x + mix, g_ref[...], b_ref[...])


def _merge(x2d, ya, yb, yc, wg, wpa, wpb, wpc, wo, g, b, tm=512):
    m, d = x2d.shape
    row = lambda i: (i, 0)
    full = lambda i: (0, 0)
    rows = lambda w: pl.BlockSpec((tm, w), row)
    whole = lambda a: pl.BlockSpec(a.shape, full, pipeline_mode=pl.Buffered(1))
    return pl.pallas_call(
        _merge_kernel,
        out_shape=jax.ShapeDtypeStruct((m, d), F32),
        grid=(m // tm,),
        in_specs=[rows(d), rows(WA), rows(WBK), rows(WC),
                  whole(wg), whole(wpa), whole(wpb), whole(wpc), whole(wo),
                  pl.BlockSpec((1, d), full), pl.BlockSpec((1, d), full)],
        out_specs=rows(d),
        compiler_params=_cparams(("parallel",)),
        name="merge_proj_ln",
    )(x2d, ya, yb, yc, wg, wpa, wpb, wpc, wo, g.reshape(1, d), b.reshape(1, d))


def _ffn_kernel(x_ref, wg_ref, wu_ref, wd_ref, g_ref, b_ref, out_ref):
    x = x_ref[...]
    xb = x.astype(BF16)
    hg = _dot(xb, wg_ref[...])
    hid = hg * _sigmoid(hg) * _dot(xb, wu_ref[...])
    ffn = _dot(hid.astype(BF16), wd_ref[...])
    out_ref[...] = _layer_norm_rows(DN_ALPHA * x + ffn, g_ref[...], b_ref[...])


def _ffn(x2d, wg, wu, wd, g, b, tm=512):
    m, d = x2d.shape
    full = lambda i: (0, 0)
    whole = lambda a: pl.BlockSpec(a.shape, full, pipeline_mode=pl.Buffered(1))
    return pl.pallas_call(
        _ffn_kernel,
        out_shape=jax.ShapeDtypeStruct((m, d), F32),
        grid=(m // tm,),
        in_specs=[pl.BlockSpec((tm, d), lambda i: (i, 0)),
                  whole(wg), whole(wu), whole(wd),
                  pl.BlockSpec((1, d), full), pl.BlockSpec((1, d), full)],
        out_specs=pl.BlockSpec((tm, d), lambda i: (i, 0)),
        compiler_params=_cparams(("parallel",)),
        name="swiglu_ln",
    )(x2d, wg, wu, wd, g.reshape(1, d), b.reshape(1, d))


def _rope_tables(seq):
    half = ROPE_DIM // 2
    inv_freq = jnp.power(ROPE_THETA, -jnp.arange(half, dtype=F32) * 2.0 / ROPE_DIM)
    dim = np.arange(LANES) % HEAD_DIM
    ang = jnp.arange(seq).astype(F32)[:, None] * inv_freq[dim % half][None, :]
    cos, sin = jnp.cos(ang), jnp.sin(ang)
    first, second = (dim < half)[None, :], ((dim >= half) & (dim < ROPE_DIM))[None, :]
    return (jnp.where(first | second, cos, 1.0), jnp.where(first, -sin, 0.0), jnp.where(second, sin, 0.0))


def _pack_w_in(w, b_f):
    sizes = (WA, WA, WA, FOX_HEADS, WBQ, WBK, WBK, WC, WC, WC, N_GATE)
    offs = np.cumsum(sizes)[:-1].tolist()
    qa, ka, va, fa, qb, kb, vb, qc, kc, vc, g = jnp.split(w, offs, axis=1)
    fa_pad = jnp.pad(fa, ((0, 0), (0, LANES - FOX_HEADS)))
    w_cat = jnp.concatenate([qa * SCALE, ka, va, fa_pad, qb * SCALE, kb, vb, qc * SCALE, kc, vc], axis=1)
    bf_pad = jnp.pad(b_f, (0, LANES - FOX_HEADS)).reshape(1, LANES)
    return w_cat.astype(BF16), g.astype(BF16), bf_pad


def kernel(x, ln_in_g, ln_in_b, w_in, b_f, w_pa, w_pb, w_pc, w_o, ln1_g, ln1_b,
           w_gate, w_up, w_down, ln2_g, ln2_b):
    bsz, seq, d = x.shape
    m = bsz * seq
    rope_tabs = _rope_tables(seq)
    h = _layer_norm(x.reshape(m, d), ln_in_g, ln_in_b)
    for l in range(DEPTH):
        w_cat, w_g, bf_pad = _pack_w_in(w_in[l], b_f[l])
        qa, ka, fvt, dq, dkv, mq, mk, mvt, km = _in_proj(h, w_cat, bf_pad, rope_tabs, bsz, seq)
        shp = lambda a: a.reshape(bsz, seq, a.shape[-1])
        ya = _fox_attention(shp(qa), shp(ka), fvt)
        yb = _dilated_attention(shp(dq), shp(dkv))
        yc = _moba_attention(shp(mq), shp(mk), mvt, km.reshape(bsz, seq // MOBA_BLOCK, WC))
        h = _merge(h, ya.reshape(m, WA), yb.reshape(m, WBK), yc.reshape(m, WC),
                   w_g, w_pa[l].astype(BF16), w_pb[l].astype(BF16), w_pc[l].astype(BF16),
                   w_o[l].astype(BF16), ln1_g[l], ln1_b[l])
        h = _ffn(h, w_gate[l].astype(BF16), w_up[l].astype(BF16), w_down[l].astype(BF16),
                 ln2_g[l], ln2_b[l])
    return h.reshape(bsz, seq, d)
```
